```python
import math
import jax, jax.numpy as jnp
from jax import lax
import numpy as np

D_MODEL = 4096
BATCH = 4
SEQ = 2048
DEPTH = 4
DEC_BATCH = 128
DEC_SEQ = 4
PAST_LEN = 16384
PAGE_SIZE = 128

W_A = D_MODEL // 2
POOL_WINDOWS = (2, 4, 8, 16)
N_POOL_GROUPS = len(POOL_WINDOWS)
POOL_GROUP = W_A // N_POOL_GROUPS
POOL_BUF = max(POOL_WINDOWS) - 1
N_HEADS = 8
W_B = D_MODEL
DV = W_B // N_HEADS
DQK = DV // 2
W_QK = N_HEADS * DQK
CHUNK = 64
EPS = 1e-6
SPLIT_SIZES = (W_A, W_A, W_QK, W_QK, W_B, W_B, W_B, D_MODEL, D_MODEL, 2 * N_HEADS)
N_IN = sum(SPLIT_SIZES)

kernel_name = "pool_mlstm_gated_hybrid_step"


def rmsnorm(x, w):
    xf = x.astype(jnp.float32)
    r = lax.rsqrt(jnp.mean(xf * xf, axis=-1, keepdims=True) + EPS)
    return (xf * r * w.astype(jnp.float32)).astype(x.dtype)


def split_proj(proj):
    offs = np.cumsum(np.array(SPLIT_SIZES))[:-1].tolist()
    return jnp.split(proj, offs, axis=-1)


def multiscale_pool(xa, buf, start_pos):
    B, T, _ = xa.shape
    ext = jnp.concatenate([buf.astype(xa.dtype), xa], axis=1)
    cs = jnp.cumsum(ext.astype(jnp.float32), axis=1)
    cs = jnp.pad(cs, ((0, 0), (1, 0), (0, 0)))
    pos = start_pos + jnp.arange(T)
    outs = []
    for g, w in enumerate(POOL_WINDOWS):
        lo, hi = g * POOL_GROUP, (g + 1) * POOL_GROUP
        win = (cs[:, POOL_BUF + 1:POOL_BUF + 1 + T, lo:hi]
               - cs[:, POOL_BUF + 1 - w:POOL_BUF + 1 - w + T, lo:hi])
        cnt = jnp.minimum(w, pos + 1).astype(jnp.float32)[None, :, None]
        outs.append(win / cnt - xa[:, :, lo:hi].astype(jnp.float32))
    pooled = jnp.stack(outs, axis=2)
    return pooled.astype(xa.dtype), ext[:, -POOL_BUF:]


def mlstm_chunk_step(carry, inp):
    C, n, m = carry
    q, k, v, ig, lf = inp
    L = q.shape[2]
    b = jnp.cumsum(lf, axis=-1)
    causal = jnp.tril(jnp.ones((L, L), dtype=bool))
    dmat = jnp.where(causal, b[..., :, None] - b[..., None, :] + ig[..., None, :], -jnp.inf)
    m_state = b + m[..., None]
    m_t = jnp.maximum(m_state, jnp.max(dmat, axis=-1))
    wts = jnp.exp(dmat - m_t[..., None])
    s_state = jnp.exp(m_state - m_t)
    qk = jnp.einsum('bhtd,bhsd->bhts', q, k) * wts
    num = (s_state[..., None] * jnp.einsum('bhtd,bhde->bhte', q, C)
           + jnp.einsum('bhts,bhse->bhte', qk, v))
    den = s_state * jnp.einsum('bhtd,bhd->bht', q, n) + jnp.sum(qk, axis=-1)
    h = num / jnp.maximum(jnp.abs(den), jnp.exp(-m_t))[..., None]
    bL = b[..., -1]
    m_new = m_t[..., -1]
    ws = jnp.exp(bL[..., None] - b + ig - m_new[..., None])
    sc = jnp.exp(bL + m - m_new)
    C_new = sc[..., None, None] * C + jnp.einsum('bhs,bhsd,bhse->bhde', ws, k, v)
    n_new = sc[..., None] * n + jnp.einsum('bhs,bhsd->bhd', ws, k)
    return (C_new, n_new, m_new), h


def mlstm(q, k, v, ig, lf, C, n, m):
    B, T, _ = q.shape
    L = CHUNK if T % CHUNK == 0 else T
    nc = T // L

    def chunks(a, d):
        return a.astype(jnp.float32).reshape(B, nc, L, N_HEADS, d).transpose(1, 0, 3, 2, 4)

    qc = chunks(q, DQK)
    kc = chunks(k, DQK) * (DQK ** -0.5)
    vc = chunks(v, DV)
    igc = ig.reshape(B, nc, L, N_HEADS).transpose(1, 0, 3, 2)
    lfc = lf.reshape(B, nc, L, N_HEADS).transpose(1, 0, 3, 2)
    carry0 = (C.astype(jnp.float32), n.astype(jnp.float32), m.astype(jnp.float32))
    (C1, n1, m1), h = lax.scan(mlstm_chunk_step, carry0, (qc, kc, vc, igc, lfc))
    h = h.transpose(1, 0, 3, 2, 4).reshape(B, T, N_HEADS, DV)
    return h, C1, n1, m1


def hybrid_layer(x, pool_buf, C, n, m, start_pos, norm_w, w_in, b_gate, w_pool,
                 pool_scale, head_norm_w, w_proj_a, w_proj_b, w_out):
    B, T, _ = x.shape
    u = rmsnorm(x, norm_w)
    proj = u @ w_in
    xa, za, q, k, v, o, zb, ga, gb, gif = split_proj(proj)
    pooled, new_buf = multiscale_pool(xa, pool_buf, start_pos)
    ha = jnp.einsum('btgc,gcd->btgd', pooled, w_pool).reshape(B, T, W_A) * pool_scale
    ha = ha * jax.nn.silu(za)
    gif = gif.astype(jnp.float32) + b_gate.astype(jnp.float32)
    ig = gif[..., :N_HEADS]
    lf = jax.nn.log_sigmoid(gif[..., N_HEADS:])
    h, C1, n1, m1 = mlstm(q, k, v, ig, lf, C, n, m)
    h = jax.nn.sigmoid(o.astype(jnp.float32)).reshape(B, T, N_HEADS, DV) * h
    h = h * lax.rsqrt(jnp.mean(h * h, axis=-1, keepdims=True) + EPS)
    h = h * head_norm_w.astype(jnp.float32).reshape(N_HEADS, DV)
    hb = h.reshape(B, T, W_B).astype(x.dtype) * jax.nn.silu(zb)
    merged = jax.nn.sigmoid(ga) * (ha @ w_proj_a) + jax.nn.sigmoid(gb) * (hb @ w_proj_b)
    return x + merged @ w_out, new_buf, C1, n1, m1


def setup_inputs(seed: int = 0) -> dict:
    key = jax.random.key(seed)
    ks = jax.random.split(key, 20)
    f32 = jnp.float32
    nrm = lambda k, s, sc: jax.random.normal(k, s, f32) * sc
    b_i = nrm(ks[0], (DEPTH, N_HEADS), 0.1) - 1.0
    b_f = nrm(ks[1], (DEPTH, N_HEADS), 0.1) + 3.0
    return {
        "x_prompt": nrm(ks[2], (BATCH, SEQ, D_MODEL), 1.0),
        "x_sample": nrm(ks[3], (DEC_BATCH, DEC_SEQ, D_MODEL), 1.0),
        "state_pool": nrm(ks[4], (DEPTH, DEC_BATCH, POOL_BUF, W_A), 1.0),
        "state_C": nrm(ks[5], (DEPTH, DEC_BATCH, N_HEADS, DQK, DV), 0.3),
        "state_n": nrm(ks[6], (DEPTH, DEC_BATCH, N_HEADS, DQK), 0.3),
        "state_m": nrm(ks[7], (DEPTH, DEC_BATCH, N_HEADS), 0.5) + 1.0,
        "norm_w": 1.0 + nrm(ks[8], (DEPTH, D_MODEL), 0.02),
        "w_in": nrm(ks[9], (DEPTH, D_MODEL, N_IN), D_MODEL ** -0.5),
        "b_gate": jnp.concatenate([b_i, b_f], axis=-1),
        "w_pool": nrm(ks[10], (DEPTH, N_POOL_GROUPS, POOL_GROUP, POOL_GROUP), POOL_GROUP ** -0.5),
        "pool_scale": 1.0 + nrm(ks[11], (DEPTH, W_A), 0.02),
        "head_norm_w": 1.0 + nrm(ks[12], (DEPTH, W_B), 0.02),
        "w_proj_a": nrm(ks[13], (DEPTH, W_A, D_MODEL), W_A ** -0.5),
        "w_proj_b": nrm(ks[14], (DEPTH, W_B, D_MODEL), W_B ** -0.5),
        "w_out": nrm(ks[15], (DEPTH, D_MODEL, D_MODEL), D_MODEL ** -0.5),
        "final_norm_w": 1.0 + nrm(ks[16], (D_MODEL,), 0.02),
    }


def reference(x_prompt, x_sample, state_pool, state_C, state_n, state_m, norm_w, w_in,
              b_gate, w_pool, pool_scale, head_norm_w, w_proj_a, w_proj_b, w_out,
              final_norm_w):
    xp, xs = x_prompt, x_sample
    pool_p0 = jnp.zeros((BATCH, POOL_BUF, W_A), x_prompt.dtype)
    C_p0 = jnp.zeros((BATCH, N_HEADS, DQK, DV), jnp.float32)
    n_p0 = jnp.zeros((BATCH, N_HEADS, DQK), jnp.float32)
    m_p0 = jnp.zeros((BATCH, N_HEADS), jnp.float32)
    pp, cp, np_, mp = [], [], [], []
    ps, cs, ns, ms = [], [], [], []
    for l in range(DEPTH):
        params = (norm_w[l], w_in[l], b_gate[l], w_pool[l], pool_scale[l], head_norm_w[l],
                  w_proj_a[l], w_proj_b[l], w_out[l])
        xp, b1, c1, n1, m1 = hybrid_layer(xp, pool_p0, C_p0, n_p0, m_p0, 0, *params)
        xs, b2, c2, n2, m2 = hybrid_layer(xs, state_pool[l], state_C[l], state_n[l],
                                          state_m[l], PAST_LEN, *params)
        pp.append(b1); cp.append(c1); np_.append(n1); mp.append(m1)
        ps.append(b2); cs.append(c2); ns.append(n2); ms.append(m2)
    y_prompt = rmsnorm(xp, final_norm_w)
    y_sample = rmsnorm(xs, final_norm_w)
    return (y_prompt, y_sample,
            jnp.stack(pp), jnp.stack(cp), jnp.stack(np_), jnp.stack(mp),
            jnp.stack(ps), jnp.stack(cs), jnp.stack(ns), jnp.stack(ms))
```

```python
import functools

import jax
import jax.numpy as jnp
from jax import lax
from jax.experimental import pallas as pl
from jax.experimental.pallas import tpu as pltpu

F32 = jnp.float32
BF16 = jnp.bfloat16

POOL_WINDOWS = (2, 4, 8, 16)
POOL_BUF = max(POOL_WINDOWS) - 1
N_HEADS = 8
EPS = 1e-6
PAST_LEN = 16384
GATE_LANES = 128
VMEM_LIMIT = 56 * 1024 * 1024


def _cparams(sem):
    return pltpu.CompilerParams(dimension_semantics=sem, vmem_limit_bytes=VMEM_LIMIT)


def _log_sigmoid(x):
    return jnp.minimum(x, 0.0) - jnp.log1p(jnp.exp(-jnp.abs(x)))


def _silu(x):
    return x * jax.nn.sigmoid(x)


def _norm_gate_kernel(x_ref, nw_ref, wg_ref, wgt_ref, bg_ref, bgt_ref, u_ref, g_ref, gt_ref):
    x = x_ref[...]
    r = lax.rsqrt(jnp.mean(x * x, axis=-1, keepdims=True) + EPS)
    ub = (x * r * nw_ref[...]).astype(BF16)
    u_ref[...] = ub
    g = jnp.dot(ub, wg_ref[...].astype(BF16), preferred_element_type=F32) + bg_ref[...]
    lane = lax.broadcasted_iota(jnp.int32, g.shape, 1)
    g_ref[...] = jnp.where(lane >= N_HEADS, _log_sigmoid(g), g)
    gt = lax.dot_general(wgt_ref[...].astype(BF16), ub, (((1,), (1,)), ((), ())),
                         preferred_element_type=F32) + bgt_ref[...]
    row = lax.broadcasted_iota(jnp.int32, gt.shape, 0)
    gt_ref[...] = jnp.where(row >= N_HEADS, _log_sigmoid(gt), gt)


def _norm_gates(x, nw, wg, wgt, bg, bgt, tm=512):
    m, d = x.shape
    return pl.pallas_call(
        _norm_gate_kernel,
        grid=(m // tm,),
        in_specs=[
            pl.BlockSpec((tm, d), lambda i: (i, 0)),
            pl.BlockSpec((1, d), lambda i: (0, 0)),
            pl.BlockSpec((d, GATE_LANES), lambda i: (0, 0)),
            pl.BlockSpec((2 * N_HEADS, d), lambda i: (0, 0)),
            pl.BlockSpec((1, GATE_LANES), lambda i: (0, 0)),
            pl.BlockSpec((2 * N_HEADS, 1), lambda i: (0, 0)),
        ],
        out_specs=[
            pl.BlockSpec((tm, d), lambda i: (i, 0)),
            pl.BlockSpec((tm, GATE_LANES), lambda i: (i, 0)),
            pl.BlockSpec((2 * N_HEADS, tm), lambda i: (0, i)),
        ],
        out_shape=[
            jax.ShapeDtypeStruct((m, d), BF16),
            jax.ShapeDtypeStruct((m, GATE_LANES), F32),
            jax.ShapeDtypeStruct((2 * N_HEADS, m), F32),
        ],
        compiler_params=_cparams(("arbitrary",)),
        name="norm_gates",
    )(x, nw, wg, wgt, bg, bgt)


def _final_norm_kernel(x_ref, nw_ref, y_ref):
    x = x_ref[...]
    r = lax.rsqrt(jnp.mean(x * x, axis=-1, keepdims=True) + EPS)
    y_ref[...] = x * r * nw_ref[...]


def _final_norm(x, nw, tm=512):
    m, d = x.shape
    return pl.pallas_call(
        _final_norm_kernel,
        grid=(m // tm,),
        in_specs=[pl.BlockSpec((tm, d), lambda i: (i, 0)),
                  pl.BlockSpec((1, d), lambda i: (0, 0))],
        out_specs=pl.BlockSpec((tm, d), lambda i: (i, 0)),
        out_shape=jax.ShapeDtypeStruct((m, d), F32),
        compiler_params=_cparams(("arbitrary",)),
        name="final_norm",
    )(x, nw)


def _in_proj_kernel(u_ref, w_ref, o_ref, wb_ref):
    @pl.when(pl.program_id(1) == 0)
    def _():
        wb_ref[...] = w_ref[...].astype(BF16)

    o_ref[...] = jnp.dot(u_ref[...], wb_ref[...], preferred_element_type=F32).astype(o_ref.dtype)


def _in_proj(u, w_all, layer, col0, ncols, out_dtype, tm=1088, tn=512):
    m, k = u.shape
    off = col0 // tn
    return pl.pallas_call(
        _in_proj_kernel,
        grid=(ncols // tn, m // tm),
        in_specs=[
            pl.BlockSpec((tm, k), lambda j, i: (i, 0)),
            pl.BlockSpec((None, k, tn), lambda j, i: (layer, 0, j + off)),
        ],
        out_specs=pl.BlockSpec((tm, tn), lambda j, i: (i, j)),
        out_shape=jax.ShapeDtypeStruct((m, ncols), out_dtype),
        scratch_shapes=[pltpu.VMEM((k, tn), BF16)],
        compiler_params=_cparams(("arbitrary", "arbitrary")),
        name="in_proj",
    )(u, w_all)


def _merge_kernel(n_prompt_blocks, ha_ref, hbp_ref, hbs_ref, ga_ref, gb_ref, wa_ref, wb_ref,
                  o_ref, wab_ref, wbb_ref):
    i = pl.program_id(1)

    @pl.when(i == 0)
    def _():
        wab_ref[...] = wa_ref[...].astype(BF16)
        wbb_ref[...] = wb_ref[...].astype(BF16)

    hb = jnp.where(i >= n_prompt_blocks, hbs_ref[...], hbp_ref[...])
    pa = jnp.dot(ha_ref[...], wab_ref[...], preferred_element_type=F32)
    pb = jnp.dot(hb, wbb_ref[...], preferred_element_type=F32)
    ga = jax.nn.sigmoid(ga_ref[...].astype(F32))
    gb = jax.nn.sigmoid(gb_ref[...].astype(F32))
    o_ref[...] = (ga * pa + gb * pb).astype(o_ref.dtype)


def _merge(ha, hb_p, hb_s, proj_b, ga_col0, gb_col0, wa_all, wb_all, layer, tn=512):
    m, ka = ha.shape
    kb = hb_p.shape[1]
    n = wa_all.shape[2]
    tm = hb_s.shape[0]
    npb = hb_p.shape[0] // tm
    return pl.pallas_call(
        functools.partial(_merge_kernel, npb),
        grid=(n // tn, m // tm),
        in_specs=[
            pl.BlockSpec((tm, ka), lambda j, i: (i, 0)),
            pl.BlockSpec((tm, kb), lambda j, i: (jnp.minimum(i, npb - 1), 0)),
            pl.BlockSpec((tm, kb), lambda j, i: (0, 0)),
            pl.BlockSpec((tm, tn), lambda j, i: (i, j + ga_col0 // tn)),
            pl.BlockSpec((tm, tn), lambda j, i: (i, j + gb_col0 // tn)),
            pl.BlockSpec((None, ka, tn), lambda j, i: (layer, 0, j)),
            pl.BlockSpec((None, kb, tn), lambda j, i: (layer, 0, j)),
        ],
        out_specs=pl.BlockSpec((tm, tn), lambda j, i: (i, j)),
        out_shape=jax.ShapeDtypeStruct((m, n), BF16),
        scratch_shapes=[pltpu.VMEM((ka, tn), BF16), pltpu.VMEM((kb, tn), BF16)],
        compiler_params=_cparams(("arbitrary", "arbitrary")),
        name="merge",
    )(ha, hb_p, hb_s, proj_b, proj_b, wa_all, wb_all)


def _out_proj_kernel(a_ref, x_ref, w_ref, o_ref, wb_ref):
    @pl.when(pl.program_id(1) == 0)
    def _():
        wb_ref[...] = w_ref[...].astype(BF16)

    o_ref[...] = x_ref[...] + jnp.dot(a_ref[...], wb_ref[...], preferred_element_type=F32)


def _out_proj(a, x, w_all, layer, tm=1088, tn=512):
    m, k = a.shape
    n = x.shape[1]
    return pl.pallas_call(
        _out_proj_kernel,
        grid=(n // tn, m // tm),
        in_specs=[
            pl.BlockSpec((tm, k), lambda j, i: (i, 0)),
            pl.BlockSpec((tm, tn), lambda j, i: (i, j)),
            pl.BlockSpec((None, k, tn), lambda j, i: (layer, 0, j)),
        ],
        out_specs=pl.BlockSpec((tm, tn), lambda j, i: (i, j)),
        out_shape=jax.ShapeDtypeStruct((m, n), F32),
        scratch_shapes=[pltpu.VMEM((k, tn), BF16)],
        compiler_params=_cparams(("arbitrary", "arbitrary")),
        name="out_proj",
    )(a, x, w_all)


def _pool_kernel(tiles_per_seq, n_prompt_tiles, n_dec_seq,
                 xa_ref, prev_ref, za_ref, sp_ref, w_ref, sc_ref,
                 ha_ref, pp_ref, ps_ref, ext_ref):
    g = pl.program_id(0)
    s = pl.program_id(1)
    tt = xa_ref.shape[0]
    wb = w_ref[...].astype(BF16)
    scale = sc_ref[...]

    def finish(pooled, za):
        y = jnp.dot(pooled.astype(BF16), wb, preferred_element_type=F32)
        return (y * scale * _silu(za)).astype(ha_ref.dtype)

    @pl.when(s < n_prompt_tiles)
    def _():
        i = s % tiles_per_seq
        xa = xa_ref[...]
        ext_ref[0:16, :] = jnp.where(i == 0, 0.0, prev_ref[...])
        ext_ref[16:16 + tt, :] = xa
        pos = i * tt + lax.broadcasted_iota(jnp.int32, (tt, 1), 0)
        for gi, w in enumerate(POOL_WINDOWS):
            @pl.when(g == gi)
            def _(w=w):
                acc = xa
                for j in range(1, w):
                    acc = acc + ext_ref[16 - j:16 - j + tt, :]
                inv = 1.0 / jnp.minimum(w, pos + 1).astype(F32)
                ha_ref[...] = finish(acc * inv - xa, za_ref[...])

        @pl.when(i == tiles_per_seq - 1)
        def _():
            pp_ref[0] = ext_ref[16 + tt - POOL_BUF:16 + tt, :]

    @pl.when(s == n_prompt_tiles)
    def _():
        nb = tt // n_dec_seq
        rows = [sp_ref[r] for r in range(POOL_BUF)]
        rows += [xa_ref[t * nb:(t + 1) * nb, :] for t in range(n_dec_seq)]
        for gi, w in enumerate(POOL_WINDOWS):
            @pl.when(g == gi)
            def _(w=w):
                for t in range(n_dec_seq):
                    acc = rows[POOL_BUF + t]
                    for j in range(1, w):
                        acc = acc + rows[POOL_BUF + t - j]
                    inv = 1.0 / float(min(w, PAST_LEN + t + 1))
                    pooled = acc * inv - rows[POOL_BUF + t]
                    ha_ref[t * nb:(t + 1) * nb, :] = finish(pooled, za_ref[t * nb:(t + 1) * nb, :])
        for r in range(POOL_BUF):
            ps_ref[r] = rows[r + n_dec_seq]


def _pool_mixer(proj_a, sp_t, w_pool, pool_scale3, layer, n_batch, seq, n_dec_seq, tt=512):
    m = proj_a.shape[0]
    w_a = proj_a.shape[1] // 2
    pg = w_a // len(POOL_WINDOWS)
    n_dec = sp_t.shape[2]
    assert n_dec * n_dec_seq == tt and seq % tt == 0 and m == n_batch * seq + tt
    tps = seq // tt
    npt = n_batch * tps
    ng = len(POOL_WINDOWS)
    return pl.pallas_call(
        functools.partial(_pool_kernel, tps, npt, n_dec_seq),
        grid=(ng, npt + 1),
        in_specs=[
            pl.BlockSpec((tt, pg), lambda g, s: (s, g)),
            pl.BlockSpec((16, pg), lambda g, s: (jnp.maximum(s * (tt // 16) - 1, 0), g)),
            pl.BlockSpec((tt, pg), lambda g, s: (s, ng + g)),
            pl.BlockSpec((None, POOL_BUF, n_dec, pg), lambda g, s: (layer, 0, 0, g)),
            pl.BlockSpec((None, None, pg, pg), lambda g, s: (layer, g, 0, 0)),
            pl.BlockSpec((None, 1, pg), lambda g, s: (layer, 0, g)),
        ],
        out_specs=[
            pl.BlockSpec((tt, pg), lambda g, s: (s, g)),
            pl.BlockSpec((1, POOL_BUF, pg), lambda g, s: (jnp.minimum(s, npt - 1) // tps, 0, g)),
            pl.BlockSpec((POOL_BUF, n_dec, pg), lambda g, s: (0, 0, g)),
        ],
        out_shape=[
            jax.ShapeDtypeStruct((m, w_a), BF16),
            jax.ShapeDtypeStruct((n_batch, POOL_BUF, w_a), F32),
            jax.ShapeDtypeStruct((POOL_BUF, n_dec, w_a), F32),
        ],
        scratch_shapes=[pltpu.VMEM((16 + tt, pg), F32)],
        compiler_params=_cparams(("arbitrary", "arbitrary")),
        name="pool_mixer",
    )(proj_a, proj_a, proj_a, sp_t, w_pool, pool_scale3)


def _gate_math(ig_c, lf_c, m0_c, mask, eye):
    def row(col):
        return jnp.sum(jnp.where(eye, col, 0.0), axis=0, keepdims=True)

    b_c = jnp.sum(jnp.where(mask, row(lf_c), 0.0), axis=1, keepdims=True)
    dmat = jnp.where(mask, b_c - row(b_c) + row(ig_c), -jnp.inf)
    m_state = b_c + m0_c
    m_t = jnp.maximum(m_state, jnp.max(dmat, axis=1, keepdims=True))
    wts = jnp.exp(dmat - m_t)
    s_state = jnp.exp(m_state - m_t)
    return b_c, m_t, wts, s_state


def _head_out(num, den, m_t, o, zb, hw):
    h = num / jnp.maximum(jnp.abs(den), jnp.exp(-m_t))
    h = jax.nn.sigmoid(o) * h
    h = h * lax.rsqrt(jnp.mean(h * h, axis=-1, keepdims=True) + EPS)
    return h * hw * _silu(zb)


def _mlstm_prompt_kernel(q_ref, k_ref, v_ref, o_ref, zb_ref, g_ref, hw_ref,
                         hb_ref, c_out, n_out, m_out, c_ref, n_ref, m_ref):
    hd = pl.program_id(1)
    c = pl.program_id(2)
    L, dqk = q_ref.shape

    @pl.when(c == 0)
    def _():
        c_ref[...] = jnp.zeros_like(c_ref)
        n_ref[...] = jnp.zeros_like(n_ref)
        m_ref[...] = jnp.zeros_like(m_ref)

    g = g_ref[...]
    lane = lax.broadcasted_iota(jnp.int32, g.shape, 1)
    ig_c = jnp.sum(jnp.where(lane == hd, g, 0.0), axis=1, keepdims=True)
    lf_c = jnp.sum(jnp.where(lane == hd + N_HEADS, g, 0.0), axis=1, keepdims=True)
    ti = lax.broadcasted_iota(jnp.int32, (L, L), 0)
    si = lax.broadcasted_iota(jnp.int32, (L, L), 1)
    m0 = m_ref[...]
    b_c, m_t, wts, s_state = _gate_math(ig_c, lf_c, m0, si <= ti, si == ti)

    q = q_ref[...]
    k = k_ref[...] * (dqk ** -0.5)
    v = v_ref[...]
    cmat = c_ref[...]
    nvec = n_ref[...]
    s_qk = lax.dot_general(q, k, (((1,), (1,)), ((), ())), preferred_element_type=F32) * wts
    num = (s_state * jnp.dot(q, cmat.astype(BF16), preferred_element_type=F32)
           + jnp.dot(s_qk.astype(BF16), v, preferred_element_type=F32))
    den = (s_state * jnp.sum(q.astype(F32) * nvec, axis=1, keepdims=True)
           + jnp.sum(s_qk, axis=1, keepdims=True))
    hb_ref[...] = _head_out(num, den, m_t, o_ref[...].astype(F32), zb_ref[...].astype(F32),
                            hw_ref[...]).astype(hb_ref.dtype)

    b_last = b_c[L - 1:L, :]
    m_new = m_t[L - 1:L, :]
    ws = jnp.exp(b_last - b_c + ig_c - m_new)
    sc = jnp.exp(b_last + m0 - m_new)
    kf = k.astype(F32)
    wv = (ws * v.astype(F32)).astype(BF16)
    c_new = sc * cmat + lax.dot_general(k, wv, (((0,), (0,)), ((), ())), preferred_element_type=F32)
    n_new = sc * nvec + jnp.sum(ws * kf, axis=0, keepdims=True)
    c_ref[...] = c_new
    n_ref[...] = n_new
    m_ref[...] = m_new

    @pl.when(c == pl.num_programs(2) - 1)
    def _():
        c_out[...] = c_new
        n_out[...] = n_new
        m_out[...] = m_new


def _mlstm_prompt(proj_b, gates, head_w3, layer, n_batch, seq, dqk, dv, cols, L=256):
    nc = seq // L
    q0, k0, v0, o0, z0 = (cols[n] for n in ("q", "k", "v", "o", "zb"))
    rows = n_batch * seq

    def rb(b, h, c):
        return b * nc + c

    return pl.pallas_call(
        _mlstm_prompt_kernel,
        grid=(n_batch, N_HEADS, nc),
        in_specs=[
            pl.BlockSpec((L, dqk), lambda b, h, c: (rb(b, h, c), q0 // dqk + h)),
            pl.BlockSpec((L, dqk), lambda b, h, c: (rb(b, h, c), k0 // dqk + h)),
            pl.BlockSpec((L, dv), lambda b, h, c: (rb(b, h, c), v0 // dv + h)),
            pl.BlockSpec((L, dv), lambda b, h, c: (rb(b, h, c), o0 // dv + h)),
            pl.BlockSpec((L, dv), lambda b, h, c: (rb(b, h, c), z0 // dv + h)),
            pl.BlockSpec((L, GATE_LANES), lambda b, h, c: (rb(b, h, c), 0)),
            pl.BlockSpec((None, 1, dv), lambda b, h, c: (layer, 0, h)),
        ],
        out_specs=[
            pl.BlockSpec((L, dv), lambda b, h, c: (rb(b, h, c), h)),
            pl.BlockSpec((None, None, dqk, dv), lambda b, h, c: (b, h, 0, 0)),
            pl.BlockSpec((None, None, 1, dqk), lambda b, h, c: (b, h, 0, 0)),
            pl.BlockSpec((None, None, 1, 1), lambda b, h, c: (b, h, 0, 0)),
        ],
        out_shape=[
            jax.ShapeDtypeStruct((rows, N_HEADS * dv), BF16),
            jax.ShapeDtypeStruct((n_batch, N_HEADS, dqk, dv), F32),
            jax.ShapeDtypeStruct((n_batch, N_HEADS, 1, dqk), F32),
            jax.ShapeDtypeStruct((n_batch, N_HEADS, 1, 1), F32),
        ],
        scratch_shapes=[pltpu.VMEM((dqk, dv), F32), pltpu.VMEM((1, dqk), F32),
                        pltpu.VMEM((1, 1), F32)],
        compiler_params=_cparams(("arbitrary", "arbitrary", "arbitrary")),
        name="mlstm_prompt",
    )(proj_b, proj_b, proj_b, proj_b, proj_b, gates, head_w3)


def _mlstm_sample_kernel(q_ref, k_ref, v_ref, o_ref, zb_ref, g_ref, m0_ref, hw_ref, c_in, n_in,
                         hb_ref, c_out, n_out, m_out):
    hd = pl.program_id(1)
    T, nb, dqk = q_ref.shape
    dv = v_ref.shape[2]
    L = T * nb

    def flat(ref):
        return ref[...].reshape(L, ref.shape[2])

    g = flat(g_ref)
    lane = lax.broadcasted_iota(jnp.int32, g.shape, 1)
    ig_c = jnp.sum(jnp.where(lane == hd, g, 0.0), axis=1, keepdims=True)
    lf_c = jnp.sum(jnp.where(lane == hd + N_HEADS, g, 0.0), axis=1, keepdims=True)
    m0_c = jnp.sum(jnp.where(lane == hd, flat(m0_ref), 0.0), axis=1, keepdims=True)
    ri = lax.broadcasted_iota(jnp.int32, (L, L), 0)
    ci = lax.broadcasted_iota(jnp.int32, (L, L), 1)
    mask = jnp.logical_and(ri % nb == ci % nb, ci <= ri)
    b_c, m_t, wts, s_state = _gate_math(ig_c, lf_c, m0_c, mask, ri == ci)

    q = flat(q_ref)
    k = flat(k_ref) * (dqk ** -0.5)
    v = flat(v_ref)
    qf = q.astype(F32)
    kf = k.astype(F32)
    n_old = n_in[...]
    s_qk = lax.dot_general(q, k, (((1,), (1,)), ((), ())), preferred_element_type=F32) * wts
    pv = jnp.dot(s_qk.astype(BF16), v, preferred_element_type=F32)
    seq_of_row = lax.broadcasted_iota(jnp.int32, (L, 1), 0) % nb
    qc = jnp.zeros((L, dv), F32)
    for j in range(nb):
        qj = jnp.where(seq_of_row == j, q, jnp.zeros_like(q))
        qc = qc + jnp.dot(qj, c_in[j].astype(BF16), preferred_element_type=F32)
    qn = jnp.sum(qf * jnp.concatenate([n_old] * T, axis=0), axis=1, keepdims=True)
    num = s_state * qc + pv
    den = s_state * qn + jnp.sum(s_qk, axis=1, keepdims=True)
    hval = _head_out(num, den, m_t, flat(o_ref).astype(F32), flat(zb_ref).astype(F32), hw_ref[...])
    hb_ref[...] = hval.astype(hb_ref.dtype).reshape(T, nb, dv)

    last = slice((T - 1) * nb, T * nb)
    b_last = b_c[last, :]
    m_new = m_t[last, :]
    sc = jnp.exp(b_last + m0_c[last, :] - m_new)
    rep = lambda a: jnp.concatenate([a] * T, axis=0)
    ws = jnp.exp(rep(b_last) - b_c + ig_c - rep(m_new))
    wv = (ws * v.astype(F32)).astype(BF16)
    wk = ws * kf
    n_upd = wk[0:nb, :]
    for t in range(1, T):
        n_upd = n_upd + wk[t * nb:(t + 1) * nb, :]
    n_out[...] = sc * n_old + n_upd
    m_out[...] = m_new
    kt = kf.T
    seq_of_col = lax.broadcasted_iota(jnp.int32, (1, L), 1) % nb
    for j in range(nb):
        ktj = jnp.where(seq_of_col == j, kt, 0.0).astype(BF16)
        c_out[j] = sc[j:j + 1, :] * c_in[j] + jnp.dot(ktj, wv, preferred_element_type=F32)


def _mlstm_sample(proj_b3, gates3, m0_3, head_w3, c_all, n_t, layer, row_blk0, dqk, dv, cols, nb=16):
    n_dec = proj_b3.shape[1]
    T = m0_3.shape[0]
    q0, k0, v0, o0, z0 = (cols[n] for n in ("q", "k", "v", "o", "zb"))
    tb = row_blk0 // T
    assert row_blk0 % T == 0

    return pl.pallas_call(
        _mlstm_sample_kernel,
        grid=(n_dec // nb, N_HEADS),
        in_specs=[
            pl.BlockSpec((T, nb, dqk), lambda i, h: (tb, i, q0 // dqk + h)),
            pl.BlockSpec((T, nb, dqk), lambda i, h: (tb, i, k0 // dqk + h)),
            pl.BlockSpec((T, nb, dv), lambda i, h: (tb, i, v0 // dv + h)),
            pl.BlockSpec((T, nb, dv), lambda i, h: (tb, i, o0 // dv + h)),
            pl.BlockSpec((T, nb, dv), lambda i, h: (tb, i, z0 // dv + h)),
            pl.BlockSpec((T, nb, GATE_LANES), lambda i, h: (tb, i, 0)),
            pl.BlockSpec((T, nb, GATE_LANES), lambda i, h: (0, i, 0)),
            pl.BlockSpec((None, 1, dv), lambda i, h: (layer, 0, h)),
            pl.BlockSpec((None, nb, None, dqk, dv), lambda i, h: (layer, i, h, 0, 0)),
            pl.BlockSpec((None, None, nb, dqk), lambda i, h: (layer, h, i, 0)),
        ],
        out_specs=[
            pl.BlockSpec((T, nb, dv), lambda i, h: (0, i, h)),
            pl.BlockSpec((nb, None, dqk, dv), lambda i, h: (i, h, 0, 0)),
            pl.BlockSpec((None, nb, dqk), lambda i, h: (h, i, 0)),
            pl.BlockSpec((None, nb, 1), lambda i, h: (h, i, 0)),
        ],
        out_shape=[
            jax.ShapeDtypeStruct((T, n_dec, N_HEADS * dv), BF16),
            jax.ShapeDtypeStruct((n_dec, N_HEADS, dqk, dv), F32),
            jax.ShapeDtypeStruct((N_HEADS, n_dec, dqk), F32),
            jax.ShapeDtypeStruct((N_HEADS, n_dec, 1), F32),
        ],
        compiler_params=_cparams(("arbitrary", "arbitrary")),
        name="mlstm_sample",
    )(proj_b3, proj_b3, proj_b3, proj_b3, proj_b3, gates3, m0_3, head_w3, c_all, n_t)


def kernel(x_prompt, x_sample, state_pool, state_C, state_n, state_m, norm_w, w_in, b_gate, w_pool,
           pool_scale, head_norm_w, w_proj_a, w_proj_b, w_out, final_norm_w):
    n_batch, seq, d = x_prompt.shape
    n_dec, dec_seq, _ = x_sample.shape
    depth = norm_w.shape[0]
    w_a = state_pool.shape[3]
    dqk, dv = state_C.shape[3], state_C.shape[4]
    w_qk, w_b = N_HEADS * dqk, N_HEADS * dv
    n_main = 2 * w_a + 2 * w_qk + 3 * w_b + 2 * d
    assert w_in.shape[2] == n_main + 2 * N_HEADS
    n_a = 2 * w_a
    names = ("q", "k", "v", "o", "zb", "ga", "gb")
    sizes = (w_qk, w_qk, w_b, w_b, w_b, d, d)
    cols, c0 = {}, 0
    for nme, sz in zip(names, sizes):
        cols[nme] = c0
        c0 += sz
    m_p = n_batch * seq
    m_s = n_dec * dec_seq
    m = m_p + m_s

    x = jnp.concatenate([x_prompt.reshape(m_p, d),
                         x_sample.transpose(1, 0, 2).reshape(m_s, d)], axis=0)
    w_gate = w_in[:, :, n_main:]
    wg = jnp.pad(w_gate, ((0, 0), (0, 0), (0, GATE_LANES - 2 * N_HEADS)))
    wgt = w_gate.transpose(0, 2, 1)
    bg = jnp.pad(b_gate, ((0, 0), (0, GATE_LANES - 2 * N_HEADS)))[:, None, :]
    bgt = b_gate[:, :, None]
    sp_t = state_pool.transpose(0, 2, 1, 3)
    n_t = state_n.transpose(0, 2, 1, 3)
    m0_all = jnp.pad(jnp.tile(state_m[:, None], (1, dec_seq, 1, 1)),
                     ((0, 0), (0, 0), (0, 0), (0, GATE_LANES - N_HEADS)))
    pool_scale3 = pool_scale[:, None, :]
    head_w3 = head_norm_w[:, None, :]
    norm_w3 = norm_w[:, None, :]

    pool_p, c_p, n_p, m_pr = [], [], [], []
    pool_s, c_s, n_s, m_sm = [], [], [], []
    for l in range(depth):
        u, gates, _ = _norm_gates(x, norm_w3[l], wg[l], wgt[l], bg[l], bgt[l])
        proj_a = _in_proj(u, w_in, l, 0, n_a, F32)
        proj_b = _in_proj(u, w_in, l, n_a, n_main - n_a, BF16)
        ha, pp, ps = _pool_mixer(proj_a, sp_t, w_pool, pool_scale3, l, n_batch, seq, dec_seq)
        hb_p, cp, np_, mp = _mlstm_prompt(proj_b, gates, head_w3, l, n_batch, seq, dqk, dv, cols)
        hb_s, cs, ns, ms = _mlstm_sample(
            proj_b.reshape(m // n_dec, n_dec, proj_b.shape[1]),
            gates.reshape(m // n_dec, n_dec, GATE_LANES),
            m0_all[l], head_w3, state_C, n_t, l, m_p // n_dec, dqk, dv, cols)
        merged = _merge(ha, hb_p, hb_s.reshape(m_s, w_b), proj_b, cols["ga"], cols["gb"],
                        w_proj_a, w_proj_b, l)
        x = _out_proj(merged, x, w_out, l)
        pool_p.append(pp); c_p.append(cp); n_p.append(np_.reshape(n_batch, N_HEADS, dqk))
        m_pr.append(mp.reshape(n_batch, N_HEADS))
        pool_s.append(ps.transpose(1, 0, 2)); c_s.append(cs); n_s.append(ns.transpose(1, 0, 2))
        m_sm.append(ms.reshape(N_HEADS, n_dec).T)

    y = _final_norm(x, final_norm_w[None, :])
    y_prompt = y[:m_p].reshape(n_batch, seq, d)
    y_sample = y[m_p:].reshape(dec_seq, n_dec, d).transpose(1, 0, 2)
    return (y_prompt, y_sample,
            jnp.stack(pool_p), jnp.stack(c_p), jnp.stack(n_p), jnp.stack(m_pr),
            jnp.stack(pool_s), jnp.stack(c_s), jnp.stack(n_s), jnp.stack(m_sm))
```

```python
import functools

import jax
import jax.numpy as jnp
from jax import lax
from jax.experimental import pallas as pl
from jax.experimental.pallas import tpu as pltpu

F32 = jnp.float32
BF16 = jnp.bfloat16

POOL_WINDOWS = (2, 4, 8, 16)
POOL_BUF = max(POOL_WINDOWS) - 1
N_HEADS = 8
EPS = 1e-6
PAST_LEN = 16384
GATE_LANES = 128
VMEM_LIMIT = 56 * 1024 * 1024


def _cparams(sem):
    return pltpu.CompilerParams(dimension_semantics=sem, vmem_limit_bytes=VMEM_LIMIT)


def _log_sigmoid(x):
    return jnp.minimum(x, 0.0) - jnp.log1p(jnp.exp(-jnp.abs(x)))


def _silu(x):
    return x * jax.nn.sigmoid(x)


def _norm_gate_kernel(x_ref, nw_ref, wg_ref, bg_ref, u_ref, g_ref):
    x = x_ref[...]
    r = lax.rsqrt(jnp.mean(x * x, axis=-1, keepdims=True) + EPS)
    ub = (x * r * nw_ref[...]).astype(BF16)
    u_ref[...] = ub
    wg = wg_ref[...].astype(BF16)
    wg = jnp.concatenate([wg, jnp.zeros((GATE_LANES - wg.shape[0], wg.shape[1]), BF16)], axis=0)
    g = lax.dot_general(ub, wg, (((1,), (1,)), ((), ())), preferred_element_type=F32) + bg_ref[...]
    lane = lax.broadcasted_iota(jnp.int32, g.shape, 1)
    g_ref[...] = jnp.where(lane >= N_HEADS, _log_sigmoid(g), g)


def _norm_gates(x, nw, w_t, layer, gate_row0, bg, tm=512):
    m, d = x.shape
    ng = 2 * N_HEADS
    assert gate_row0 % ng == 0 and w_t.shape[1] - gate_row0 == ng
    return pl.pallas_call(
        _norm_gate_kernel,
        grid=(m // tm,),
        in_specs=[
            pl.BlockSpec((tm, d), lambda i: (i, 0)),
            pl.BlockSpec((1, d), lambda i: (0, 0)),
            pl.BlockSpec((None, ng, d), lambda i: (layer, gate_row0 // ng, 0)),
            pl.BlockSpec((1, GATE_LANES), lambda i: (0, 0)),
        ],
        out_specs=[
            pl.BlockSpec((tm, d), lambda i: (i, 0)),
            pl.BlockSpec((tm, GATE_LANES), lambda i: (i, 0)),
        ],
        out_shape=[
            jax.ShapeDtypeStruct((m, d), BF16),
            jax.ShapeDtypeStruct((m, GATE_LANES), F32),
        ],
        compiler_params=_cparams(("arbitrary",)),
        name="norm_gates",
    )(x, nw, w_t, bg)


def _final_norm_kernel(x_ref, nw_ref, y_ref):
    x = x_ref[...]
    r = lax.rsqrt(jnp.mean(x * x, axis=-1, keepdims=True) + EPS)
    y_ref[...] = x * r * nw_ref[...]


def _final_norm(x, nw, tm=512):
    m, d = x.shape
    return pl.pallas_call(
        _final_norm_kernel,
        grid=(m // tm,),
        in_specs=[pl.BlockSpec((tm, d), lambda i: (i, 0)),
                  pl.BlockSpec((1, d), lambda i: (0, 0))],
        out_specs=pl.BlockSpec((tm, d), lambda i: (i, 0)),
        out_shape=jax.ShapeDtypeStruct((m, d), F32),
        compiler_params=_cparams(("arbitrary",)),
        name="final_norm",
    )(x, nw)


def _in_proj_kernel(u_ref, w_ref, o_ref, wb_ref):
    @pl.when(pl.program_id(1) == 0)
    def _():
        wb_ref[...] = w_ref[...].astype(BF16)

    o_ref[...] = lax.dot_general(u_ref[...], wb_ref[...], (((1,), (1,)), ((), ())),
                                 preferred_element_type=F32).astype(o_ref.dtype)


def _in_proj(u, w_t, layer, col0, ncols, out_dtype, tm=1088, tn=512):
    m, k = u.shape
    off = col0 // tn
    return pl.pallas_call(
        _in_proj_kernel,
        grid=(ncols // tn, m // tm),
        in_specs=[
            pl.BlockSpec((tm, k), lambda j, i: (i, 0)),
            pl.BlockSpec((None, tn, k), lambda j, i: (layer, j + off, 0)),
        ],
        out_specs=pl.BlockSpec((tm, tn), lambda j, i: (i, j)),
        out_shape=jax.ShapeDtypeStruct((m, ncols), out_dtype),
        scratch_shapes=[pltpu.VMEM((tn, k), BF16)],
        compiler_params=_cparams(("arbitrary", "arbitrary")),
        name="in_proj",
    )(u, w_t)


def _merge_kernel(n_prompt_blocks, ha_ref, hbp_ref, hbs_ref, ga_ref, gb_ref, wa_ref, wb_ref,
                  o_ref, wab_ref, wbb_ref):
    i = pl.program_id(1)

    @pl.when(i == 0)
    def _():
        wab_ref[...] = wa_ref[...].astype(BF16)
        wbb_ref[...] = wb_ref[...].astype(BF16)

    hb = jnp.where(i >= n_prompt_blocks, hbs_ref[...], hbp_ref[...])
    pa = jnp.dot(ha_ref[...], wab_ref[...], preferred_element_type=F32)
    pb = jnp.dot(hb, wbb_ref[...], preferred_element_type=F32)
    ga = jax.nn.sigmoid(ga_ref[...].astype(F32))
    gb = jax.nn.sigmoid(gb_ref[...].astype(F32))
    o_ref[...] = (ga * pa + gb * pb).astype(o_ref.dtype)


def _merge(ha, hb_p, hb_s, proj_b, ga_col0, gb_col0, wa_all, wb_all, layer, tn=512):
    m, ka = ha.shape
    kb = hb_p.shape[1]
    n = wa_all.shape[2]
    tm = hb_s.shape[0]
    npb = hb_p.shape[0] // tm
    return pl.pallas_call(
        functools.partial(_merge_kernel, npb),
        grid=(n // tn, m // tm),
        in_specs=[
            pl.BlockSpec((tm, ka), lambda j, i: (i, 0)),
            pl.BlockSpec((tm, kb), lambda j, i: (jnp.minimum(i, npb - 1), 0)),
            pl.BlockSpec((tm, kb), lambda j, i: (0, 0)),
            pl.BlockSpec((tm, tn), lambda j, i: (i, j + ga_col0 // tn)),
            pl.BlockSpec((tm, tn), lambda j, i: (i, j + gb_col0 // tn)),
            pl.BlockSpec((None, ka, tn), lambda j, i: (layer, 0, j)),
            pl.BlockSpec((None, kb, tn), lambda j, i: (layer, 0, j)),
        ],
        out_specs=pl.BlockSpec((tm, tn), lambda j, i: (i, j)),
        out_shape=jax.ShapeDtypeStruct((m, n), BF16),
        scratch_shapes=[pltpu.VMEM((ka, tn), BF16), pltpu.VMEM((kb, tn), BF16)],
        compiler_params=_cparams(("arbitrary", "arbitrary")),
        name="merge",
    )(ha, hb_p, hb_s, proj_b, proj_b, wa_all, wb_all)


def _out_proj_kernel(a_ref, x_ref, w_ref, o_ref, wb_ref):
    @pl.when(pl.program_id(1) == 0)
    def _():
        wb_ref[...] = w_ref[...].astype(BF16)

    o_ref[...] = x_ref[...] + jnp.dot(a_ref[...], wb_ref[...], preferred_element_type=F32)


def _out_proj(a, x, w_all, layer, tm=1088, tn=512):
    m, k = a.shape
    n = x.shape[1]
    return pl.pallas_call(
        _out_proj_kernel,
        grid=(n // tn, m // tm),
        in_specs=[
            pl.BlockSpec((tm, k), lambda j, i: (i, 0)),
            pl.BlockSpec((tm, tn), lambda j, i: (i, j)),
            pl.BlockSpec((None, k, tn), lambda j, i: (layer, 0, j)),
        ],
        out_specs=pl.BlockSpec((tm, tn), lambda j, i: (i, j)),
        out_shape=jax.ShapeDtypeStruct((m, n), F32),
        scratch_shapes=[pltpu.VMEM((k, tn), BF16)],
        compiler_params=_cparams(("arbitrary", "arbitrary")),
        name="out_proj",
    )(a, x, w_all)


def _pool_kernel(tiles_per_seq, n_prompt_tiles, n_dec_seq,
                 xa_ref, prev_ref, za_ref, sp_ref, w_ref, sc_ref,
                 ha_ref, pp_ref, ps_ref, ext_ref):
    g = pl.program_id(0)
    s = pl.program_id(1)
    tt = xa_ref.shape[0]
    wb = w_ref[...].astype(BF16)
    scale = sc_ref[...]

    def finish(pooled, za):
        y = jnp.dot(pooled.astype(BF16), wb, preferred_element_type=F32)
        return (y * scale * _silu(za)).astype(ha_ref.dtype)

    @pl.when(s < n_prompt_tiles)
    def _():
        i = s % tiles_per_seq
        xa = xa_ref[...]
        ext_ref[0:16, :] = jnp.where(i == 0, 0.0, prev_ref[...])
        ext_ref[16:16 + tt, :] = xa
        pos = i * tt + lax.broadcasted_iota(jnp.int32, (tt, 1), 0)
        for gi, w in enumerate(POOL_WINDOWS):
            @pl.when(g == gi)
            def _(w=w):
                acc = xa
                for j in range(1, w):
                    acc = acc + ext_ref[16 - j:16 - j + tt, :]
                inv = 1.0 / jnp.minimum(w, pos + 1).astype(F32)
                ha_ref[...] = finish(acc * inv - xa, za_ref[...])

        @pl.when(i == tiles_per_seq - 1)
        def _():
            pp_ref[0] = ext_ref[16 + tt - POOL_BUF:16 + tt, :]

    @pl.when(s == n_prompt_tiles)
    def _():
        nb = tt // n_dec_seq
        rows = [sp_ref[r] for r in range(POOL_BUF)]
        rows += [xa_ref[t * nb:(t + 1) * nb, :] for t in range(n_dec_seq)]
        for gi, w in enumerate(POOL_WINDOWS):
            @pl.when(g == gi)
            def _(w=w):
                for t in range(n_dec_seq):
                    acc = rows[POOL_BUF + t]
                    for j in range(1, w):
                        acc = acc + rows[POOL_BUF + t - j]
                    inv = 1.0 / float(min(w, PAST_LEN + t + 1))
                    pooled = acc * inv - rows[POOL_BUF + t]
                    ha_ref[t * nb:(t + 1) * nb, :] = finish(pooled, za_ref[t * nb:(t + 1) * nb, :])
        for r in range(POOL_BUF):
            ps_ref[r] = rows[r + n_dec_seq]


def _pool_mixer(proj_a, sp_t, w_pool, pool_scale3, layer, n_batch, seq, n_dec_seq, tt=512):
    m = proj_a.shape[0]
    w_a = proj_a.shape[1] // 2
    pg = w_a // len(POOL_WINDOWS)
    n_dec = sp_t.shape[2]
    assert n_dec * n_dec_seq == tt and seq % tt == 0 and m == n_batch * seq + tt
    tps = seq // tt
    npt = n_batch * tps
    ng = len(POOL_WINDOWS)
    return pl.pallas_call(
        functools.partial(_pool_kernel, tps, npt, n_dec_seq),
        grid=(ng, npt + 1),
        in_specs=[
            pl.BlockSpec((tt, pg), lambda g, s: (s, g)),
            pl.BlockSpec((16, pg), lambda g, s: (jnp.maximum(s * (tt // 16) - 1, 0), g)),
            pl.BlockSpec((tt, pg), lambda g, s: (s, ng + g)),
            pl.BlockSpec((None, POOL_BUF, n_dec, pg), lambda g, s: (layer, 0, 0, g)),
            pl.BlockSpec((None, None, pg, pg), lambda g, s: (layer, g, 0, 0)),
            pl.BlockSpec((None, 1, pg), lambda g, s: (layer, 0, g)),
        ],
        out_specs=[
            pl.BlockSpec((tt, pg), lambda g, s: (s, g)),
            pl.BlockSpec((1, POOL_BUF, pg), lambda g, s: (jnp.minimum(s, npt - 1) // tps, 0, g)),
            pl.BlockSpec((POOL_BUF, n_dec, pg), lambda g, s: (0, 0, g)),
        ],
        out_shape=[
            jax.ShapeDtypeStruct((m, w_a), BF16),
            jax.ShapeDtypeStruct((n_batch, POOL_BUF, w_a), F32),
            jax.ShapeDtypeStruct((POOL_BUF, n_dec, w_a), F32),
        ],
        scratch_shapes=[pltpu.VMEM((16 + tt, pg), F32)],
        compiler_params=_cparams(("arbitrary", "arbitrary")),
        name="pool_mixer",
    )(proj_a, proj_a, proj_a, sp_t, w_pool, pool_scale3)


def _gate_math(ig_c, lf_c, m0_c, mask, eye):
    def row(col):
        return jnp.sum(jnp.where(eye, col, 0.0), axis=0, keepdims=True)

    b_c = jnp.sum(jnp.where(mask, row(lf_c), 0.0), axis=1, keepdims=True)
    dmat = jnp.where(mask, b_c - row(b_c) + row(ig_c), -jnp.inf)
    m_state = b_c + m0_c
    m_t = jnp.maximum(m_state, jnp.max(dmat, axis=1, keepdims=True))
    wts = jnp.exp(dmat - m_t)
    s_state = jnp.exp(m_state - m_t)
    return b_c, m_t, wts, s_state


def _head_out(num, den, m_t, o, zb, hw):
    h = num / jnp.maximum(jnp.abs(den), jnp.exp(-m_t))
    h = jax.nn.sigmoid(o) * h
    h = h * lax.rsqrt(jnp.mean(h * h, axis=-1, keepdims=True) + EPS)
    return h * hw * _silu(zb)


def _mlstm_prompt_kernel(q_ref, k_ref, v_ref, o_ref, zb_ref, g_ref, hw_ref,
                         hb_ref, c_out, n_out, m_out, c_ref, n_ref, m_ref):
    hd = pl.program_id(1)
    c = pl.program_id(2)
    L, dqk = q_ref.shape

    @pl.when(c == 0)
    def _():
        c_ref[...] = jnp.zeros_like(c_ref)
        n_ref[...] = jnp.zeros_like(n_ref)
        m_ref[...] = jnp.zeros_like(m_ref)

    g = g_ref[...]
    lane = lax.broadcasted_iota(jnp.int32, g.shape, 1)
    ig_c = jnp.sum(jnp.where(lane == hd, g, 0.0), axis=1, keepdims=True)
    lf_c = jnp.sum(jnp.where(lane == hd + N_HEADS, g, 0.0), axis=1, keepdims=True)
    ti = lax.broadcasted_iota(jnp.int32, (L, L), 0)
    si = lax.broadcasted_iota(jnp.int32, (L, L), 1)
    m0 = m_ref[...]
    b_c, m_t, wts, s_state = _gate_math(ig_c, lf_c, m0, si <= ti, si == ti)

    q = q_ref[...]
    k = k_ref[...] * (dqk ** -0.5)
    v = v_ref[...]
    cmat = c_ref[...]
    nvec = n_ref[...]
    s_qk = lax.dot_general(q, k, (((1,), (1,)), ((), ())), preferred_element_type=F32) * wts
    num = (s_state * jnp.dot(q, cmat.astype(BF16), preferred_element_type=F32)
           + jnp.dot(s_qk.astype(BF16), v, preferred_element_type=F32))
    den = (s_state * jnp.sum(q.astype(F32) * nvec, axis=1, keepdims=True)
           + jnp.sum(s_qk, axis=1, keepdims=True))
    hb_ref[...] = _head_out(num, den, m_t, o_ref[...].astype(F32), zb_ref[...].astype(F32),
                            hw_ref[...]).astype(hb_ref.dtype)

    b_last = b_c[L - 1:L, :]
    m_new = m_t[L - 1:L, :]
    ws = jnp.exp(b_last - b_c + ig_c - m_new)
    sc = jnp.exp(b_last + m0 - m_new)
    kf = k.astype(F32)
    wv = (ws * v.astype(F32)).astype(BF16)
    c_new = sc * cmat + lax.dot_general(k, wv, (((0,), (0,)), ((), ())), preferred_element_type=F32)
    n_new = sc * nvec + jnp.sum(ws * kf, axis=0, keepdims=True)
    c_ref[...] = c_new
    n_ref[...] = n_new
    m_ref[...] = m_new

    @pl.when(c == pl.num_programs(2) - 1)
    def _():
        c_out[...] = c_new
        n_out[...] = n_new
        m_out[...] = m_new


def _mlstm_prompt(proj_b, gates, head_w3, layer, n_batch, seq, dqk, dv, cols, L=256):
    nc = seq // L
    q0, k0, v0, o0, z0 = (cols[n] for n in ("q", "k", "v", "o", "zb"))
    rows = n_batch * seq

    def rb(b, h, c):
        return b * nc + c

    return pl.pallas_call(
        _mlstm_prompt_kernel,
        grid=(n_batch, N_HEADS, nc),
        in_specs=[
            pl.BlockSpec((L, dqk), lambda b, h, c: (rb(b, h, c), q0 // dqk + h)),
            pl.BlockSpec((L, dqk), lambda b, h, c: (rb(b, h, c), k0 // dqk + h)),
            pl.BlockSpec((L, dv), lambda b, h, c: (rb(b, h, c), v0 // dv + h)),
            pl.BlockSpec((L, dv), lambda b, h, c: (rb(b, h, c), o0 // dv + h)),
            pl.BlockSpec((L, dv), lambda b, h, c: (rb(b, h, c), z0 // dv + h)),
            pl.BlockSpec((L, GATE_LANES), lambda b, h, c: (rb(b, h, c), 0)),
            pl.BlockSpec((None, 1, dv), lambda b, h, c: (layer, 0, h)),
        ],
        out_specs=[
            pl.BlockSpec((L, dv), lambda b, h, c: (rb(b, h, c), h)),
            pl.BlockSpec((None, None, dqk, dv), lambda b, h, c: (b, h, 0, 0)),
            pl.BlockSpec((None, None, 1, dqk), lambda b, h, c: (b, h, 0, 0)),
            pl.BlockSpec((None, None, 1, 1), lambda b, h, c: (b, h, 0, 0)),
        ],
        out_shape=[
            jax.ShapeDtypeStruct((rows, N_HEADS * dv), BF16),
            jax.ShapeDtypeStruct((n_batch, N_HEADS, dqk, dv), F32),
            jax.ShapeDtypeStruct((n_batch, N_HEADS, 1, dqk), F32),
            jax.ShapeDtypeStruct((n_batch, N_HEADS, 1, 1), F32),
        ],
        scratch_shapes=[pltpu.VMEM((dqk, dv), F32), pltpu.VMEM((1, dqk), F32),
                        pltpu.VMEM((1, 1), F32)],
        compiler_params=_cparams(("arbitrary", "arbitrary", "arbitrary")),
        name="mlstm_prompt",
    )(proj_b, proj_b, proj_b, proj_b, proj_b, gates, head_w3)


def _mlstm_sample_kernel(q_ref, k_ref, v_ref, o_ref, zb_ref, g_ref, m0_ref, hw_ref, c_in, n_in,
                         c_acc_hbm, hb_ref, c_out, n_out, m_out):
    del c_acc_hbm
    hd = pl.program_id(1)
    T, nb, dqk = q_ref.shape
    dv = v_ref.shape[2]
    L = T * nb

    def flat(ref):
        return ref[...].reshape(L, ref.shape[2])

    g = flat(g_ref)
    lane = lax.broadcasted_iota(jnp.int32, g.shape, 1)
    ig_c = jnp.sum(jnp.where(lane == hd, g, 0.0), axis=1, keepdims=True)
    lf_c = jnp.sum(jnp.where(lane == hd + N_HEADS, g, 0.0), axis=1, keepdims=True)
    m0_c = jnp.sum(jnp.where(lane == hd, flat(m0_ref), 0.0), axis=1, keepdims=True)
    ri = lax.broadcasted_iota(jnp.int32, (L, L), 0)
    ci = lax.broadcasted_iota(jnp.int32, (L, L), 1)
    mask = jnp.logical_and(ri % nb == ci % nb, ci <= ri)
    b_c, m_t, wts, s_state = _gate_math(ig_c, lf_c, m0_c, mask, ri == ci)

    q = flat(q_ref)
    k = flat(k_ref) * (dqk ** -0.5)
    v = flat(v_ref)
    qf = q.astype(F32)
    kf = k.astype(F32)
    n_old = n_in[...]
    s_qk = lax.dot_general(q, k, (((1,), (1,)), ((), ())), preferred_element_type=F32) * wts
    pv = jnp.dot(s_qk.astype(BF16), v, preferred_element_type=F32)
    seq_of_row = lax.broadcasted_iota(jnp.int32, (L, 1), 0) % nb
    qc = jnp.zeros((L, dv), F32)
    for j in range(nb):
        qj = jnp.where(seq_of_row == j, q, jnp.zeros_like(q))
        qc = qc + jnp.dot(qj, c_in[j].astype(BF16), preferred_element_type=F32)
    qn = jnp.sum(qf * jnp.concatenate([n_old] * T, axis=0), axis=1, keepdims=True)
    num = s_state * qc + pv
    den = s_state * qn + jnp.sum(s_qk, axis=1, keepdims=True)
    hval = _head_out(num, den, m_t, flat(o_ref).astype(F32), flat(zb_ref).astype(F32), hw_ref[...])
    hb_ref[...] = hval.astype(hb_ref.dtype).reshape(T, nb, dv)

    last = slice((T - 1) * nb, T * nb)
    b_last = b_c[last, :]
    m_new = m_t[last, :]
    sc = jnp.exp(b_last + m0_c[last, :] - m_new)
    rep = lambda a: jnp.concatenate([a] * T, axis=0)
    ws = jnp.exp(rep(b_last) - b_c + ig_c - rep(m_new))
    wv = (ws * v.astype(F32)).astype(BF16)
    wk = ws * kf
    n_upd = wk[0:nb, :]
    for t in range(1, T):
        n_upd = n_upd + wk[t * nb:(t + 1) * nb, :]
    n_out[...] = sc * n_old + n_upd
    m_out[...] = m_new
    kt = kf.T
    seq_of_col = lax.broadcasted_iota(jnp.int32, (1, L), 1) % nb
    for j in range(nb):
        ktj = jnp.where(seq_of_col == j, kt, 0.0).astype(BF16)
        c_out[j] = sc[j:j + 1, :] * c_in[j] + jnp.dot(ktj, wv, preferred_element_type=F32)


def _mlstm_sample(proj_b3, gates3, m0_3, head_w3, c_all, n_t, c_acc, layer, row_blk0, dqk, dv, cols,
                  nb=16):
    n_dec = proj_b3.shape[1]
    T = m0_3.shape[0]
    q0, k0, v0, o0, z0 = (cols[n] for n in ("q", "k", "v", "o", "zb"))
    tb = row_blk0 // T
    assert row_blk0 % T == 0

    return pl.pallas_call(
        _mlstm_sample_kernel,
        grid=(n_dec // nb, N_HEADS),
        in_specs=[
            pl.BlockSpec((T, nb, dqk), lambda i, h: (tb, i, q0 // dqk + h)),
            pl.BlockSpec((T, nb, dqk), lambda i, h: (tb, i, k0 // dqk + h)),
            pl.BlockSpec((T, nb, dv), lambda i, h: (tb, i, v0 // dv + h)),
            pl.BlockSpec((T, nb, dv), lambda i, h: (tb, i, o0 // dv + h)),
            pl.BlockSpec((T, nb, dv), lambda i, h: (tb, i, z0 // dv + h)),
            pl.BlockSpec((T, nb, GATE_LANES), lambda i, h: (tb, i, 0)),
            pl.BlockSpec((T, nb, GATE_LANES), lambda i, h: (0, i, 0)),
            pl.BlockSpec((None, 1, dv), lambda i, h: (layer, 0, h)),
            pl.BlockSpec((None, nb, None, dqk, dv), lambda i, h: (layer, i, h, 0, 0)),
            pl.BlockSpec((None, None, nb, dqk), lambda i, h: (layer, h, i, 0)),
            pl.BlockSpec(memory_space=pl.ANY),
        ],
        out_specs=[
            pl.BlockSpec((T, nb, dv), lambda i, h: (0, i, h)),
            pl.BlockSpec((None, nb, None, dqk, dv), lambda i, h: (layer, i, h, 0, 0)),
            pl.BlockSpec((None, nb, dqk), lambda i, h: (h, i, 0)),
            pl.BlockSpec((None, nb, 1), lambda i, h: (h, i, 0)),
        ],
        out_shape=[
            jax.ShapeDtypeStruct((T, n_dec, N_HEADS * dv), BF16),
            jax.ShapeDtypeStruct(c_acc.shape, F32),
            jax.ShapeDtypeStruct((N_HEADS, n_dec, dqk), F32),
            jax.ShapeDtypeStruct((N_HEADS, n_dec, 1), F32),
        ],
        input_output_aliases={10: 1},
        compiler_params=_cparams(("arbitrary", "arbitrary")),
        name="mlstm_sample",
    )(proj_b3, proj_b3, proj_b3, proj_b3, proj_b3, gates3, m0_3, head_w3, c_all, n_t, c_acc)


def kernel(x_prompt, x_sample, state_pool, state_C, state_n, state_m, norm_w, w_in, b_gate, w_pool,
           pool_scale, head_norm_w, w_proj_a, w_proj_b, w_out, final_norm_w):
    n_batch, seq, d = x_prompt.shape
    n_dec, dec_seq, _ = x_sample.shape
    depth = norm_w.shape[0]
    w_a = state_pool.shape[3]
    dqk, dv = state_C.shape[3], state_C.shape[4]
    w_qk, w_b = N_HEADS * dqk, N_HEADS * dv
    n_main = 2 * w_a + 2 * w_qk + 3 * w_b + 2 * d
    assert w_in.shape[2] == n_main + 2 * N_HEADS
    n_a = 2 * w_a
    names = ("q", "k", "v", "o", "zb", "ga", "gb")
    sizes = (w_qk, w_qk, w_b, w_b, w_b, d, d)
    cols, c0 = {}, 0
    for nme, sz in zip(names, sizes):
        cols[nme] = c0
        c0 += sz
    m_p = n_batch * seq
    m_s = n_dec * dec_seq
    m = m_p + m_s

    x = jnp.concatenate([x_prompt.reshape(m_p, d),
                         x_sample.transpose(1, 0, 2).reshape(m_s, d)], axis=0)
    w_t = w_in.transpose(0, 2, 1)
    bg = jnp.pad(b_gate, ((0, 0), (0, GATE_LANES - 2 * N_HEADS)))[:, None, :]
    sp_t = state_pool.transpose(0, 2, 1, 3)
    n_t = state_n.transpose(0, 2, 1, 3)
    m0_all = jnp.pad(jnp.tile(state_m[:, None], (1, dec_seq, 1, 1)),
                     ((0, 0), (0, 0), (0, 0), (0, GATE_LANES - N_HEADS)))
    pool_scale3 = pool_scale[:, None, :]
    head_w3 = head_norm_w[:, None, :]
    norm_w3 = norm_w[:, None, :]

    pool_p, c_p, n_p, m_pr = [], [], [], []
    pool_s, n_s, m_sm = [], [], []
    c_s = jnp.zeros(state_C.shape, F32)
    for l in range(depth):
        u, gates = _norm_gates(x, norm_w3[l], w_t, l, n_main, bg[l])
        proj_a = _in_proj(u, w_t, l, 0, n_a, F32)
        proj_b = _in_proj(u, w_t, l, n_a, n_main - n_a, BF16)
        ha, pp, ps = _pool_mixer(proj_a, sp_t, w_pool, pool_scale3, l, n_batch, seq, dec_seq)
        hb_p, cp, np_, mp = _mlstm_prompt(proj_b, gates, head_w3, l, n_batch, seq, dqk, dv, cols)
        hb_s, c_s, ns, ms = _mlstm_sample(
            proj_b.reshape(m // n_dec, n_dec, proj_b.shape[1]),
            gates.reshape(m // n_dec, n_dec, GATE_LANES),
            m0_all[l], head_w3, state_C, n_t, c_s, l, m_p // n_dec, dqk, dv, cols)
        merged = _merge(ha, hb_p, hb_s.reshape(m_s, w_b), proj_b, cols["ga"], cols["gb"],
                        w_proj_a, w_proj_b, l)
        x = _out_proj(merged, x, w_out, l)
        pool_p.append(pp); c_p.append(cp); n_p.append(np_.reshape(n_batch, N_HEADS, dqk))
        m_pr.append(mp.reshape(n_batch, N_HEADS))
        pool_s.append(ps.transpose(1, 0, 2)); n_s.append(ns.transpose(1, 0, 2))
        m_sm.append(ms.reshape(N_HEADS, n_dec).T)

    y = _final_norm(x, final_norm_w[None, :])
    y_prompt = y[:m_p].reshape(n_batch, seq, d)
    y_sample = y[m_p:].reshape(dec_seq, n_dec, d).transpose(1, 0, 2)
    return (y_prompt, y_sample,
            jnp.stack(pool_p), jnp.stack(c_p), jnp.stack(n_p), jnp.stack(m_pr),
            jnp.stack(pool_s), c_s, jnp.stack(n_s), jnp.stack(m_sm))
```

```python
import functools

import jax
import jax.numpy as jnp
from jax import lax
from jax.experimental import pallas as pl
from jax.experimental.pallas import tpu as pltpu

F32 = jnp.float32
BF16 = jnp.bfloat16

POOL_WINDOWS = (2, 4, 8, 16)
POOL_BUF = max(POOL_WINDOWS) - 1
N_HEADS = 8
EPS = 1e-6
PAST_LEN = 16384
GATE_LANES = 128
VMEM_LIMIT = 56 * 1024 * 1024


def _cparams(sem):
    return pltpu.CompilerParams(dimension_semantics=sem, vmem_limit_bytes=VMEM_LIMIT)


def _log_sigmoid(x):
    return jnp.minimum(x, 0.0) - jnp.log1p(jnp.exp(-jnp.abs(x)))


def _sigmoid(x):
    return 0.5 * jnp.tanh(0.5 * x) + 0.5


def _silu(x):
    return x * _sigmoid(x)


def _norm_gate_kernel(x_ref, nw_ref, wg_ref, bg_ref, u_ref, g_ref):
    x = x_ref[...]
    r = lax.rsqrt(jnp.mean(x * x, axis=-1, keepdims=True) + EPS)
    ub = (x * r * nw_ref[...]).astype(BF16)
    u_ref[...] = ub
    wg = wg_ref[...].astype(BF16)
    wg = jnp.concatenate([wg, jnp.zeros((GATE_LANES - wg.shape[0], wg.shape[1]), BF16)], axis=0)
    g = lax.dot_general(ub, wg, (((1,), (1,)), ((), ())), preferred_element_type=F32) + bg_ref[...]
    lane = lax.broadcasted_iota(jnp.int32, g.shape, 1)
    g_ref[...] = jnp.where(lane >= N_HEADS, _log_sigmoid(g), g)


def _norm_gates(x, nw, w_t, layer, gate_row0, bg, tm=512):
    m, d = x.shape
    ng = 2 * N_HEADS
    assert gate_row0 % ng == 0 and w_t.shape[1] - gate_row0 == ng
    return pl.pallas_call(
        _norm_gate_kernel,
        grid=(m // tm,),
        in_specs=[
            pl.BlockSpec((tm, d), lambda i: (i, 0)),
            pl.BlockSpec((1, d), lambda i: (0, 0)),
            pl.BlockSpec((None, ng, d), lambda i: (layer, gate_row0 // ng, 0)),
            pl.BlockSpec((1, GATE_LANES), lambda i: (0, 0)),
        ],
        out_specs=[
            pl.BlockSpec((tm, d), lambda i: (i, 0)),
            pl.BlockSpec((tm, GATE_LANES), lambda i: (i, 0)),
        ],
        out_shape=[
            jax.ShapeDtypeStruct((m, d), BF16),
            jax.ShapeDtypeStruct((m, GATE_LANES), F32),
        ],
        compiler_params=_cparams(("arbitrary",)),
        name="norm_gates",
    )(x, nw, w_t, bg)


def _final_norm_kernel(n_prompt_blocks, x_ref, nw_ref, yp_ref, ys_ref):
    i = pl.program_id(0)
    x = x_ref[...]
    r = lax.rsqrt(jnp.mean(x * x, axis=-1, keepdims=True) + EPS)
    y = x * r * nw_ref[...]

    @pl.when(i < n_prompt_blocks)
    def _():
        yp_ref[...] = y

    @pl.when(i == n_prompt_blocks)
    def _():
        ys_ref[...] = y


def _final_norm(x, nw, m_p):
    m, d = x.shape
    tm = m - m_p
    assert m_p % tm == 0
    npb = m_p // tm
    return pl.pallas_call(
        functools.partial(_final_norm_kernel, npb),
        grid=(npb + 1,),
        in_specs=[pl.BlockSpec((tm, d), lambda i: (i, 0)),
                  pl.BlockSpec((1, d), lambda i: (0, 0))],
        out_specs=[pl.BlockSpec((tm, d), lambda i: (jnp.minimum(i, npb - 1), 0)),
                   pl.BlockSpec((tm, d), lambda i: (0, 0))],
        out_shape=[jax.ShapeDtypeStruct((m_p, d), F32), jax.ShapeDtypeStruct((tm, d), F32)],
        compiler_params=_cparams(("arbitrary",)),
        name="final_norm",
    )(x, nw)


def _in_proj_kernel(u_ref, w_ref, o_ref, wb_ref):
    @pl.when(pl.program_id(1) == 0)
    def _():
        wb_ref[...] = w_ref[...].astype(BF16)

    o_ref[...] = lax.dot_general(u_ref[...], wb_ref[...], (((1,), (1,)), ((), ())),
                                 preferred_element_type=F32).astype(o_ref.dtype)


def _in_proj(u, w_t, layer, col0, ncols, out_dtype, tm=544, tn=1024):
    m, k = u.shape
    off = col0 // tn
    return pl.pallas_call(
        _in_proj_kernel,
        grid=(ncols // tn, m // tm),
        in_specs=[
            pl.BlockSpec((tm, k), lambda j, i: (i, 0)),
            pl.BlockSpec((None, tn, k), lambda j, i: (layer, j + off, 0)),
        ],
        out_specs=pl.BlockSpec((tm, tn), lambda j, i: (i, j)),
        out_shape=jax.ShapeDtypeStruct((m, ncols), out_dtype),
        scratch_shapes=[pltpu.VMEM((tn, k), BF16)],
        compiler_params=_cparams(("arbitrary", "arbitrary")),
        name="in_proj",
    )(u, w_t)


def _merge_kernel(n_prompt_blocks, ha_ref, hbp_ref, hbs_ref, ga_ref, gb_ref, wa_ref, wb_ref,
                  o_ref, wab_ref, wbb_ref):
    i = pl.program_id(1)

    @pl.when(i == 0)
    def _():
        wab_ref[...] = wa_ref[...].astype(BF16)
        wbb_ref[...] = wb_ref[...].astype(BF16)

    hb = jnp.where(i >= n_prompt_blocks, hbs_ref[...], hbp_ref[...])
    pa = jnp.dot(ha_ref[...], wab_ref[...], preferred_element_type=F32)
    pb = jnp.dot(hb, wbb_ref[...], preferred_element_type=F32)
    ga = _sigmoid(ga_ref[...].astype(F32))
    gb = _sigmoid(gb_ref[...].astype(F32))
    o_ref[...] = (ga * pa + gb * pb).astype(o_ref.dtype)


def _merge(ha, hb_p, hb_s, proj_b, ga_col0, gb_col0, wa_all, wb_all, layer, tn=512):
    m, ka = ha.shape
    kb = hb_p.shape[1]
    n = wa_all.shape[2]
    tm = hb_s.shape[0]
    npb = hb_p.shape[0] // tm
    return pl.pallas_call(
        functools.partial(_merge_kernel, npb),
        grid=(n // tn, m // tm),
        in_specs=[
            pl.BlockSpec((tm, ka), lambda j, i: (i, 0)),
            pl.BlockSpec((tm, kb), lambda j, i: (jnp.minimum(i, npb - 1), 0)),
            pl.BlockSpec((tm, kb), lambda j, i: (0, 0)),
            pl.BlockSpec((tm, tn), lambda j, i: (i, j + ga_col0 // tn)),
            pl.BlockSpec((tm, tn), lambda j, i: (i, j + gb_col0 // tn)),
            pl.BlockSpec((None, ka, tn), lambda j, i: (layer, 0, j)),
            pl.BlockSpec((None, kb, tn), lambda j, i: (layer, 0, j)),
        ],
        out_specs=pl.BlockSpec((tm, tn), lambda j, i: (i, j)),
        out_shape=jax.ShapeDtypeStruct((m, n), BF16),
        scratch_shapes=[pltpu.VMEM((ka, tn), BF16), pltpu.VMEM((kb, tn), BF16)],
        compiler_params=_cparams(("arbitrary", "arbitrary")),
        name="merge",
    )(ha, hb_p, hb_s, proj_b, proj_b, wa_all, wb_all)


def _out_proj_kernel(a_ref, x_ref, w_ref, o_ref, wb_ref):
    @pl.when(pl.program_id(1) == 0)
    def _():
        wb_ref[...] = w_ref[...].astype(BF16)

    o_ref[...] = x_ref[...] + jnp.dot(a_ref[...], wb_ref[...], preferred_element_type=F32)


def _out_proj(a, x, w_all, layer, tm=1088, tn=512):
    m, k = a.shape
    n = x.shape[1]
    return pl.pallas_call(
        _out_proj_kernel,
        grid=(n // tn, m // tm),
        in_specs=[
            pl.BlockSpec((tm, k), lambda j, i: (i, 0)),
            pl.BlockSpec((tm, tn), lambda j, i: (i, j)),
            pl.BlockSpec((None, k, tn), lambda j, i: (layer, 0, j)),
        ],
        out_specs=pl.BlockSpec((tm, tn), lambda j, i: (i, j)),
        out_shape=jax.ShapeDtypeStruct((m, n), F32),
        scratch_shapes=[pltpu.VMEM((k, tn), BF16)],
        compiler_params=_cparams(("arbitrary", "arbitrary")),
        name="out_proj",
    )(a, x, w_all)


def _pool_kernel(tiles_per_seq, n_prompt_tiles, n_dec_seq,
                 xa_ref, prev_ref, za_ref, sp_ref, w_ref, sc_ref,
                 ha_ref, pp_ref, ps_ref, ext_ref):
    g = pl.program_id(0)
    s = pl.program_id(1)
    tt = xa_ref.shape[0]
    wb = w_ref[...].astype(BF16)
    scale = sc_ref[...]

    def finish(pooled, za):
        y = jnp.dot(pooled.astype(BF16), wb, preferred_element_type=F32)
        return (y * scale * _silu(za)).astype(ha_ref.dtype)

    @pl.when(s < n_prompt_tiles)
    def _():
        i = s % tiles_per_seq
        xa = xa_ref[...]
        ext_ref[0:16, :] = jnp.where(i == 0, 0.0, prev_ref[...])
        ext_ref[16:16 + tt, :] = xa
        pos = i * tt + lax.broadcasted_iota(jnp.int32, (tt, 1), 0)
        for gi, w in enumerate(POOL_WINDOWS):
            @pl.when(g == gi)
            def _(w=w):
                acc = xa
                for j in range(1, w):
                    acc = acc + ext_ref[16 - j:16 - j + tt, :]
                inv = 1.0 / jnp.minimum(w, pos + 1).astype(F32)
                ha_ref[...] = finish(acc * inv - xa, za_ref[...])

        @pl.when(i == tiles_per_seq - 1)
        def _():
            pp_ref[0] = ext_ref[16 + tt - POOL_BUF:16 + tt, :]

    @pl.when(s == n_prompt_tiles)
    def _():
        nb = tt // n_dec_seq
        rows = [sp_ref[r] for r in range(POOL_BUF)]
        rows += [xa_ref[t * nb:(t + 1) * nb, :] for t in range(n_dec_seq)]
        for gi, w in enumerate(POOL_WINDOWS):
            @pl.when(g == gi)
            def _(w=w):
                for t in range(n_dec_seq):
                    acc = rows[POOL_BUF + t]
                    for j in range(1, w):
                        acc = acc + rows[POOL_BUF + t - j]
                    inv = 1.0 / float(min(w, PAST_LEN + t + 1))
                    pooled = acc * inv - rows[POOL_BUF + t]
                    ha_ref[t * nb:(t + 1) * nb, :] = finish(pooled, za_ref[t * nb:(t + 1) * nb, :])
        for r in range(POOL_BUF):
            ps_ref[r] = rows[r + n_dec_seq]


def _pool_mixer(proj_a, sp_t, w_pool, pool_scale3, layer, n_batch, seq, n_dec_seq, tt=512):
    m = proj_a.shape[0]
    w_a = proj_a.shape[1] // 2
    pg = w_a // len(POOL_WINDOWS)
    n_dec = sp_t.shape[2]
    assert n_dec * n_dec_seq == tt and seq % tt == 0 and m == n_batch * seq + tt
    tps = seq // tt
    npt = n_batch * tps
    ng = len(POOL_WINDOWS)
    return pl.pallas_call(
        functools.partial(_pool_kernel, tps, npt, n_dec_seq),
        grid=(ng, npt + 1),
        in_specs=[
            pl.BlockSpec((tt, pg), lambda g, s: (s, g)),
            pl.BlockSpec((16, pg), lambda g, s: (jnp.maximum(s * (tt // 16) - 1, 0), g)),
            pl.BlockSpec((tt, pg), lambda g, s: (s, ng + g)),
            pl.BlockSpec((None, POOL_BUF, n_dec, pg), lambda g, s: (layer, 0, 0, g)),
            pl.BlockSpec((None, None, pg, pg), lambda g, s: (layer, g, 0, 0)),
            pl.BlockSpec((None, 1, pg), lambda g, s: (layer, 0, g)),
        ],
        out_specs=[
            pl.BlockSpec((tt, pg), lambda g, s: (s, g)),
            pl.BlockSpec((1, POOL_BUF, pg), lambda g, s: (jnp.minimum(s, npt - 1) // tps, 0, g)),
            pl.BlockSpec((POOL_BUF, n_dec, pg), lambda g, s: (0, 0, g)),
        ],
        out_shape=[
            jax.ShapeDtypeStruct((m, w_a), BF16),
            jax.ShapeDtypeStruct((n_batch, POOL_BUF, w_a), F32),
            jax.ShapeDtypeStruct((POOL_BUF, n_dec, w_a), F32),
        ],
        scratch_shapes=[pltpu.VMEM((16 + tt, pg), F32)],
        compiler_params=_cparams(("arbitrary", "arbitrary")),
        name="pool_mixer",
    )(proj_a, proj_a, proj_a, sp_t, w_pool, pool_scale3)


SAMPLE_SUBSTEPS = 4
SAMPLE_SEQS = 16


def _gate_math(ig_c, ig_r, lf_r, m0_c, mask, to_row):
    b_c = jnp.sum(jnp.where(mask, lf_r, 0.0), axis=1, keepdims=True)
    dmat = jnp.where(mask, b_c - to_row(b_c) + ig_r, -jnp.inf)
    m_state = b_c + m0_c
    m_t = jnp.maximum(m_state, jnp.max(dmat, axis=1, keepdims=True))
    wts = jnp.exp(dmat - m_t)
    s_state = jnp.exp(m_state - m_t)
    return b_c, m_t, wts, s_state


def _head_gates(h, o, zb, hw):
    h = _sigmoid(o) * h
    h = h * lax.rsqrt(jnp.mean(h * h, axis=-1, keepdims=True) + EPS)
    return h * hw * _silu(zb)


def _mlstm_kernel(n_sub, sq_ref, sk_ref, sv_ref, so_ref, sz_ref, sg_ref, sm0_ref, shw_ref, sc_in,
                  sn_in, c_acc_hbm,
                  q_ref, k_ref, v_ref, o_ref, zb_ref, g_ref, hw_ref,
                  hbs_ref, sc_out, sn_out, sm_out, hb_ref, c_out, n_out, m_out,
                  c_ref, n_ref, m_ref, gt_ref, kt_ref, wv_ref, scl_ref, qc_ref, a_ref, p_ref):
    del c_acc_hbm
    hd = pl.program_id(1)
    c = pl.program_id(2)
    nc = pl.num_programs(2)
    L, dqk = q_ref.shape

    @pl.when(c == 0)
    def _():
        c_ref[...] = jnp.zeros_like(c_ref)
        n_ref[...] = jnp.zeros_like(n_ref)
        m_ref[...] = jnp.zeros_like(m_ref)

    T, nb, _ = sq_ref.shape
    dv = sv_ref.shape[2]
    Ls = T * nb
    per_sub = nb // n_sub
    step = (pl.program_id(0) * pl.num_programs(1) + hd) * nc + c
    sub = step % n_sub
    shd = (step // n_sub) % N_HEADS

    def flat(ref):
        return ref[...].reshape(Ls, ref.shape[2])

    sq = flat(sq_ref)

    @pl.when(sub == 0)
    def _():
        sg = flat(sg_ref)
        slane = lax.broadcasted_iota(jnp.int32, sg.shape, 1)
        sig_c = jnp.sum(jnp.where(slane == shd, sg, 0.0), axis=1, keepdims=True)
        slf_c = jnp.sum(jnp.where(slane == shd + N_HEADS, sg, 0.0), axis=1, keepdims=True)
        sm0_c = jnp.sum(jnp.where(slane == shd, flat(sm0_ref), 0.0), axis=1, keepdims=True)
        ri = lax.broadcasted_iota(jnp.int32, (Ls, Ls), 0)
        ci = lax.broadcasted_iota(jnp.int32, (Ls, Ls), 1)
        smask = jnp.logical_and(ri % nb == ci % nb, ci <= ri)
        eye = ri == ci

        def s_to_row(col):
            return jnp.sum(jnp.where(eye, col, 0.0), axis=0, keepdims=True)

        sb_c, sm_t, swts, ss_state = _gate_math(sig_c, s_to_row(sig_c), s_to_row(slf_c), sm0_c,
                                                smask, s_to_row)
        sk = flat(sk_ref) * (dqk ** -0.5)
        sv = flat(sv_ref)
        skf = sk.astype(F32)
        n_old = sn_in[...]
        sqk = lax.dot_general(sq, sk, (((1,), (1,)), ((), ())), preferred_element_type=F32) * swts
        pv = jnp.dot(sqk.astype(BF16), sv, preferred_element_type=F32)
        qn = jnp.sum(sq.astype(F32) * jnp.concatenate([n_old] * T, axis=0), axis=1, keepdims=True)
        sden = ss_state * qn + jnp.sum(sqk, axis=1, keepdims=True)
        inv = 1.0 / jnp.maximum(jnp.abs(sden), jnp.exp(-sm_t))
        a_ref[...] = ss_state * inv
        p_ref[...] = pv * inv
        qc_ref[...] = jnp.zeros_like(qc_ref)

        last = slice((T - 1) * nb, T * nb)
        sb_last = sb_c[last, :]
        sm_new = sm_t[last, :]
        scl = jnp.exp(sb_last + sm0_c[last, :] - sm_new)
        rep = lambda x: jnp.concatenate([x] * T, axis=0)
        sws = jnp.exp(rep(sb_last) - sb_c + sig_c - rep(sm_new))
        wv_ref[...] = (sws * sv.astype(F32)).astype(BF16)
        wk = sws * skf
        n_upd = wk[0:nb, :]
        for t in range(1, T):
            n_upd = n_upd + wk[t * nb:(t + 1) * nb, :]
        sn_out[...] = scl * n_old + n_upd
        sm_out[...] = sm_new
        scl_ref[...] = scl
        kt_ref[...] = skf.T.astype(BF16)

    g = g_ref[...]
    lane = lax.broadcasted_iota(jnp.int32, g.shape, 1)
    ig_c = jnp.sum(jnp.where(lane == hd, g, 0.0), axis=1, keepdims=True)
    gt_ref[...] = g.T
    ig_r = gt_ref[pl.ds(hd, 1), :]
    lf_r = gt_ref[pl.ds(hd + N_HEADS, 1), :]
    ti = lax.broadcasted_iota(jnp.int32, (L, L), 0)
    si = lax.broadcasted_iota(jnp.int32, (L, L), 1)
    m0 = m_ref[...]

    def to_row(col):
        return jnp.broadcast_to(col, (L, GATE_LANES)).T[0:1, :]

    b_c, m_t, wts, s_state = _gate_math(ig_c, ig_r, lf_r, m0, si <= ti, to_row)

    q = q_ref[...]
    k = k_ref[...] * (dqk ** -0.5)
    v = v_ref[...]
    cmat = c_ref[...]
    nvec = n_ref[...]
    s_qk = lax.dot_general(q, k, (((1,), (1,)), ((), ())), preferred_element_type=F32) * wts
    num = (s_state * jnp.dot(q, cmat.astype(BF16), preferred_element_type=F32)
           + jnp.dot(s_qk.astype(BF16), v, preferred_element_type=F32))
    den = (s_state * jnp.sum(q.astype(F32) * nvec, axis=1, keepdims=True)
           + jnp.sum(s_qk, axis=1, keepdims=True))
    hval = num / jnp.maximum(jnp.abs(den), jnp.exp(-m_t))
    hb_ref[...] = _head_gates(hval, o_ref[...].astype(F32), zb_ref[...].astype(F32),
                              hw_ref[...]).astype(hb_ref.dtype)

    b_last = b_c[L - 1:L, :]
    m_new = m_t[L - 1:L, :]
    ws = jnp.exp(b_last - b_c + ig_c - m_new)
    sc = jnp.exp(b_last + m0 - m_new)
    wv = (ws * v.astype(F32)).astype(BF16)
    c_new = sc * cmat + lax.dot_general(k, wv, (((0,), (0,)), ((), ())), preferred_element_type=F32)
    n_new = sc * nvec + jnp.sum(ws * k.astype(F32), axis=0, keepdims=True)
    c_ref[...] = c_new
    n_ref[...] = n_new
    m_ref[...] = m_new

    seq_of_row = lax.broadcasted_iota(jnp.int32, (Ls, 1), 0) % nb
    seq_of_col = lax.broadcasted_iota(jnp.int32, (1, Ls), 1) % nb
    kt = kt_ref[...]
    swv = wv_ref[...]
    qc = qc_ref[...]
    for jl in range(per_sub):
        j = sub * per_sub + jl
        cj = sc_in[jl]
        qj = jnp.where(seq_of_row == j, sq, jnp.zeros_like(sq))
        qc = qc + jnp.dot(qj, cj.astype(BF16), preferred_element_type=F32)
        ktj = jnp.where(seq_of_col == j, kt, jnp.zeros_like(kt))
        sc_out[jl] = scl_ref[pl.ds(j, 1), :] * cj + jnp.dot(ktj, swv, preferred_element_type=F32)
    qc_ref[...] = qc

    @pl.when(c == nc - 1)
    def _():
        c_out[...] = c_new
        n_out[...] = n_new
        m_out[...] = m_new

    @pl.when(sub == n_sub - 1)
    def _():
        hs = a_ref[...] * qc + p_ref[...]
        hs = _head_gates(hs, flat(so_ref).astype(F32), flat(sz_ref).astype(F32), shw_ref[...])
        hbs_ref[...] = hs.astype(hbs_ref.dtype).reshape(T, nb, dv)


def _mlstm(proj_b, gates, m0_3, head_w3, c_all, n_t, c_acc, layer, n_batch, seq, dqk, dv, cols,
           L=256):
    n_dec = n_t.shape[2]
    T = m0_3.shape[0]
    H, G, nbs = N_HEADS, SAMPLE_SUBSTEPS, SAMPLE_SEQS
    nc = seq // L
    rows = n_batch * seq
    m_tot = proj_b.shape[0]
    assert m_tot == rows + T * n_dec and rows % (T * n_dec) == 0 and nc % G == 0
    assert n_batch * H * nc == (n_dec // nbs) * H * G
    q0, k0, v0, o0, z0 = (cols[n] for n in ("q", "k", "v", "o", "zb"))
    proj_b3 = proj_b.reshape(m_tot // n_dec, n_dec, proj_b.shape[1])
    gates3 = gates.reshape(m_tot // n_dec, n_dec, GATE_LANES)
    tb = rows // n_dec // T
    per_sub = nbs // G

    def rb(b, h, c):
        return b * nc + c

    def sblk(b, h, c):
        return (b * H + h) * (nc // G) + c // G

    def si(b, h, c):
        return sblk(b, h, c) // H

    def sh(b, h, c):
        return sblk(b, h, c) % H

    def cidx(b, h, c):
        return (layer, si(b, h, c) * G + c % G, sh(b, h, c), 0, 0)

    return pl.pallas_call(
        functools.partial(_mlstm_kernel, G),
        grid=(n_batch, H, nc),
        in_specs=[
            pl.BlockSpec((T, nbs, dqk), lambda b, h, c: (tb, si(b, h, c), q0 // dqk + sh(b, h, c))),
            pl.BlockSpec((T, nbs, dqk), lambda b, h, c: (tb, si(b, h, c), k0 // dqk + sh(b, h, c))),
            pl.BlockSpec((T, nbs, dv), lambda b, h, c: (tb, si(b, h, c), v0 // dv + sh(b, h, c))),
            pl.BlockSpec((T, nbs, dv), lambda b, h, c: (tb, si(b, h, c), o0 // dv + sh(b, h, c))),
            pl.BlockSpec((T, nbs, dv), lambda b, h, c: (tb, si(b, h, c), z0 // dv + sh(b, h, c))),
            pl.BlockSpec((T, nbs, GATE_LANES), lambda b, h, c: (tb, si(b, h, c), 0)),
            pl.BlockSpec((T, nbs, GATE_LANES), lambda b, h, c: (0, si(b, h, c), 0)),
            pl.BlockSpec((None, 1, dv), lambda b, h, c: (layer, 0, sh(b, h, c))),
            pl.BlockSpec((None, per_sub, None, dqk, dv), cidx),
            pl.BlockSpec((None, None, nbs, dqk), lambda b, h, c: (layer, sh(b, h, c), si(b, h, c), 0)),
            pl.BlockSpec(memory_space=pl.ANY),
            pl.BlockSpec((L, dqk), lambda b, h, c: (rb(b, h, c), q0 // dqk + h)),
            pl.BlockSpec((L, dqk), lambda b, h, c: (rb(b, h, c), k0 // dqk + h)),
            pl.BlockSpec((L, dv), lambda b, h, c: (rb(b, h, c), v0 // dv + h)),
            pl.BlockSpec((L, dv), lambda b, h, c: (rb(b, h, c), o0 // dv + h)),
            pl.BlockSpec((L, dv), lambda b, h, c: (rb(b, h, c), z0 // dv + h)),
            pl.BlockSpec((L, GATE_LANES), lambda b, h, c: (rb(b, h, c), 0)),
            pl.BlockSpec((None, 1, dv), lambda b, h, c: (layer, 0, h)),
        ],
        out_specs=[
            pl.BlockSpec((T, nbs, dv), lambda b, h, c: (0, si(b, h, c), sh(b, h, c))),
            pl.BlockSpec((None, per_sub, None, dqk, dv), cidx),
            pl.BlockSpec((None, nbs, dqk), lambda b, h, c: (sh(b, h, c), si(b, h, c), 0)),
            pl.BlockSpec((None, nbs, 1), lambda b, h, c: (sh(b, h, c), si(b, h, c), 0)),
            pl.BlockSpec((L, dv), lambda b, h, c: (rb(b, h, c), h)),
            pl.BlockSpec((None, None, dqk, dv), lambda b, h, c: (b, h, 0, 0)),
            pl.BlockSpec((None, None, 1, dqk), lambda b, h, c: (b, h, 0, 0)),
            pl.BlockSpec((None, None, 1, 1), lambda b, h, c: (b, h, 0, 0)),
        ],
        out_shape=[
            jax.ShapeDtypeStruct((T, n_dec, H * dv), BF16),
            jax.ShapeDtypeStruct(c_acc.shape, F32),
            jax.ShapeDtypeStruct((H, n_dec, dqk), F32),
            jax.ShapeDtypeStruct((H, n_dec, 1), F32),
            jax.ShapeDtypeStruct((rows, H * dv), BF16),
            jax.ShapeDtypeStruct((n_batch, H, dqk, dv), F32),
            jax.ShapeDtypeStruct((n_batch, H, 1, dqk), F32),
            jax.ShapeDtypeStruct((n_batch, H, 1, 1), F32),
        ],
        scratch_shapes=[
            pltpu.VMEM((dqk, dv), F32), pltpu.VMEM((1, dqk), F32), pltpu.VMEM((1, 1), F32),
            pltpu.VMEM((GATE_LANES, L), F32),
            pltpu.VMEM((dqk, T * nbs), BF16), pltpu.VMEM((T * nbs, dv), BF16),
            pltpu.VMEM((nbs, 1), F32), pltpu.VMEM((T * nbs, dv), F32),
            pltpu.VMEM((T * nbs, 1), F32), pltpu.VMEM((T * nbs, dv), F32),
        ],
        input_output_aliases={10: 1},
        compiler_params=_cparams(("arbitrary", "arbitrary", "arbitrary")),
        name="mlstm",
    )(proj_b3, proj_b3, proj_b3, proj_b3, proj_b3, gates3, m0_3, head_w3, c_all, n_t, c_acc,
      proj_b, proj_b, proj_b, proj_b, proj_b, gates, head_w3)


def kernel(x_prompt, x_sample, state_pool, state_C, state_n, state_m, norm_w, w_in, b_gate, w_pool,
           pool_scale, head_norm_w, w_proj_a, w_proj_b, w_out, final_norm_w):
    n_batch, seq, d = x_prompt.shape
    n_dec, dec_seq, _ = x_sample.shape
    depth = norm_w.shape[0]
    w_a = state_pool.shape[3]
    dqk, dv = state_C.shape[3], state_C.shape[4]
    w_qk, w_b = N_HEADS * dqk, N_HEADS * dv
    n_main = 2 * w_a + 2 * w_qk + 3 * w_b + 2 * d
    assert w_in.shape[2] == n_main + 2 * N_HEADS
    n_a = 2 * w_a
    names = ("q", "k", "v", "o", "zb", "ga", "gb")
    sizes = (w_qk, w_qk, w_b, w_b, w_b, d, d)
    cols, c0 = {}, 0
    for nme, sz in zip(names, sizes):
        cols[nme] = c0
        c0 += sz
    m_p = n_batch * seq
    m_s = n_dec * dec_seq
    m = m_p + m_s

    x = jnp.concatenate([x_prompt.reshape(m_p, d),
                         x_sample.transpose(1, 0, 2).reshape(m_s, d)], axis=0)
    w_t = w_in.transpose(0, 2, 1)
    bg = jnp.pad(b_gate, ((0, 0), (0, GATE_LANES - 2 * N_HEADS)))[:, None, :]
    sp_t = state_pool.transpose(0, 2, 1, 3)
    n_t = state_n.transpose(0, 2, 1, 3)
    m0_all = jnp.pad(jnp.tile(state_m[:, None], (1, dec_seq, 1, 1)),
                     ((0, 0), (0, 0), (0, 0), (0, GATE_LANES - N_HEADS)))
    pool_scale3 = pool_scale[:, None, :]
    head_w3 = head_norm_w[:, None, :]
    norm_w3 = norm_w[:, None, :]

    pool_p, c_p, n_p, m_pr = [], [], [], []
    pool_s, n_s, m_sm = [], [], []
    c_s = jnp.zeros(state_C.shape, F32)
    for l in range(depth):
        u, gates = _norm_gates(x, norm_w3[l], w_t, l, n_main, bg[l])
        proj_a = _in_proj(u, w_t, l, 0, n_a, F32, tm=1088, tn=512)
        proj_b = _in_proj(u, w_t, l, n_a, n_main - n_a, BF16)
        ha, pp, ps = _pool_mixer(proj_a, sp_t, w_pool, pool_scale3, l, n_batch, seq, dec_seq)
        hb_s, c_s, ns, ms, hb_p, cp, np_, mp = _mlstm(
            proj_b, gates, m0_all[l], head_w3, state_C, n_t, c_s, l, n_batch, seq, dqk, dv, cols)
        merged = _merge(ha, hb_p, hb_s.reshape(m_s, w_b), proj_b, cols["ga"], cols["gb"],
                        w_proj_a, w_proj_b, l)
        x = _out_proj(merged, x, w_out, l)
        pool_p.append(pp); c_p.append(cp); n_p.append(np_.reshape(n_batch, N_HEADS, dqk))
        m_pr.append(mp.reshape(n_batch, N_HEADS))
        pool_s.append(ps.transpose(1, 0, 2)); n_s.append(ns.transpose(1, 0, 2))
        m_sm.append(ms.reshape(N_HEADS, n_dec).T)

    y_p, y_s = _final_norm(x, final_norm_w[None, :], m_p)
    y_prompt = y_p.reshape(n_batch, seq, d)
    y_sample = y_s.reshape(dec_seq, n_dec, d).transpose(1, 0, 2)
    return (y_prompt, y_sample,
            jnp.stack(pool_p), jnp.stack(c_p), jnp.stack(n_p), jnp.stack(m_pr),
            jnp.stack(pool_s), c_s, jnp.stack(n_s), jnp.stack(m_sm))
```

```python
import functools

import jax
import jax.numpy as jnp
from jax import lax
from jax.experimental import pallas as pl
from jax.experimental.pallas import tpu as pltpu

F32 = jnp.float32
BF16 = jnp.bfloat16

POOL_WINDOWS = (2, 4, 8, 16)
POOL_BUF = max(POOL_WINDOWS) - 1
POOL_PAD = 16
N_HEADS = 8
EPS = 1e-6
PAST_LEN = 16384
GATE_LANES = 128
VMEM_LIMIT = 56 * 1024 * 1024


def _cparams(sem):
    return pltpu.CompilerParams(dimension_semantics=sem, vmem_limit_bytes=VMEM_LIMIT)


def _log_sigmoid(x):
    return jnp.minimum(x, 0.0) - jnp.log1p(jnp.exp(-jnp.abs(x)))


def _sigmoid(x):
    return 0.5 * jnp.tanh(0.5 * x) + 0.5


def _silu(x):
    return x * _sigmoid(x)


def _norm_gate_kernel(x_ref, nw_ref, wg_ref, bg_ref, u_ref, g_ref):
    x = x_ref[...]
    r = lax.rsqrt(jnp.mean(x * x, axis=-1, keepdims=True) + EPS)
    ub = (x * r * nw_ref[...]).astype(BF16)
    u_ref[...] = ub
    wg = wg_ref[...].astype(BF16)
    wg = jnp.concatenate([wg, jnp.zeros((GATE_LANES - wg.shape[0], wg.shape[1]), BF16)], axis=0)
    g = lax.dot_general(ub, wg, (((1,), (1,)), ((), ())), preferred_element_type=F32) + bg_ref[...]
    lane = lax.broadcasted_iota(jnp.int32, g.shape, 1)
    g_ref[...] = jnp.where(lane >= N_HEADS, _log_sigmoid(g), g)


def _norm_gates(x, nw, w_t, layer, gate_row0, bg, tm=512):
    m, d = x.shape
    ng = 2 * N_HEADS
    assert gate_row0 % ng == 0 and w_t.shape[1] - gate_row0 == ng
    return pl.pallas_call(
        _norm_gate_kernel,
        grid=(m // tm,),
        in_specs=[
            pl.BlockSpec((tm, d), lambda i: (i, 0)),
            pl.BlockSpec((1, d), lambda i: (0, 0)),
            pl.BlockSpec((None, ng, d), lambda i: (layer, gate_row0 // ng, 0)),
            pl.BlockSpec((1, GATE_LANES), lambda i: (0, 0)),
        ],
        out_specs=[
            pl.BlockSpec((tm, d), lambda i: (i, 0)),
            pl.BlockSpec((tm, GATE_LANES), lambda i: (i, 0)),
        ],
        out_shape=[
            jax.ShapeDtypeStruct((m, d), BF16),
            jax.ShapeDtypeStruct((m, GATE_LANES), F32),
        ],
        compiler_params=_cparams(("arbitrary",)),
        name="norm_gates",
    )(x, nw, w_t, bg)


def _final_norm_kernel(n_prompt_blocks, x_ref, nw_ref, yp_ref, ys_ref):
    i = pl.program_id(0)
    x = x_ref[...]
    r = lax.rsqrt(jnp.mean(x * x, axis=-1, keepdims=True) + EPS)
    y = x * r * nw_ref[...]

    @pl.when(i < n_prompt_blocks)
    def _():
        yp_ref[...] = y

    @pl.when(i == n_prompt_blocks)
    def _():
        ys_ref[...] = y


def _final_norm(x, nw, m_p):
    m, d = x.shape
    tm = m - m_p
    assert m_p % tm == 0
    npb = m_p // tm
    return pl.pallas_call(
        functools.partial(_final_norm_kernel, npb),
        grid=(npb + 1,),
        in_specs=[pl.BlockSpec((tm, d), lambda i: (i, 0)),
                  pl.BlockSpec((1, d), lambda i: (0, 0))],
        out_specs=[pl.BlockSpec((tm, d), lambda i: (jnp.minimum(i, npb - 1), 0)),
                   pl.BlockSpec((tm, d), lambda i: (0, 0))],
        out_shape=[jax.ShapeDtypeStruct((m_p, d), F32), jax.ShapeDtypeStruct((tm, d), F32)],
        compiler_params=_cparams(("arbitrary",)),
        name="final_norm",
    )(x, nw)


def _in_proj_kernel(u_ref, w_ref, o_ref, wb_ref):
    @pl.when(pl.program_id(1) == 0)
    def _():
        wb_ref[...] = w_ref[...].astype(BF16)

    o_ref[...] = lax.dot_general(u_ref[...], wb_ref[...], (((1,), (1,)), ((), ())),
                                 preferred_element_type=F32).astype(o_ref.dtype)


def _in_proj(u, w_t, layer, col0, ncols, out_dtype, tm=544, tn=1024):
    m, k = u.shape
    off = col0 // tn
    return pl.pallas_call(
        _in_proj_kernel,
        grid=(ncols // tn, m // tm),
        in_specs=[
            pl.BlockSpec((tm, k), lambda j, i: (i, 0)),
            pl.BlockSpec((None, tn, k), lambda j, i: (layer, j + off, 0)),
        ],
        out_specs=pl.BlockSpec((tm, tn), lambda j, i: (i, j)),
        out_shape=jax.ShapeDtypeStruct((m, ncols), out_dtype),
        scratch_shapes=[pltpu.VMEM((tn, k), BF16)],
        compiler_params=_cparams(("arbitrary", "arbitrary")),
        name="in_proj",
    )(u, w_t)


def _merge_kernel(n_prompt_blocks, ha_ref, hbp_ref, hbs_ref, ga_ref, gb_ref, wa_ref, wb_ref,
                  o_ref, wab_ref, wbb_ref):
    i = pl.program_id(1)

    @pl.when(i == 0)
    def _():
        wab_ref[...] = wa_ref[...].astype(BF16)
        wbb_ref[...] = wb_ref[...].astype(BF16)

    hb = jnp.where(i >= n_prompt_blocks, hbs_ref[...], hbp_ref[...])
    pa = jnp.dot(ha_ref[...], wab_ref[...], preferred_element_type=F32)
    pb = jnp.dot(hb, wbb_ref[...], preferred_element_type=F32)
    ga = _sigmoid(ga_ref[...].astype(F32))
    gb = _sigmoid(gb_ref[...].astype(F32))
    o_ref[...] = (ga * pa + gb * pb).astype(o_ref.dtype)


def _merge(ha, hb_p, hb_s, proj_b, ga_col0, gb_col0, wa_all, wb_all, layer, tn=512):
    m, ka = ha.shape
    kb = hb_p.shape[1]
    n = wa_all.shape[2]
    tm = hb_s.shape[0]
    npb = hb_p.shape[0] // tm
    return pl.pallas_call(
        functools.partial(_merge_kernel, npb),
        grid=(n // tn, m // tm),
        in_specs=[
            pl.BlockSpec((tm, ka), lambda j, i: (i, 0)),
            pl.BlockSpec((tm, kb), lambda j, i: (jnp.minimum(i, npb - 1), 0)),
            pl.BlockSpec((tm, kb), lambda j, i: (0, 0)),
            pl.BlockSpec((tm, tn), lambda j, i: (i, j + ga_col0 // tn)),
            pl.BlockSpec((tm, tn), lambda j, i: (i, j + gb_col0 // tn)),
            pl.BlockSpec((None, ka, tn), lambda j, i: (layer, 0, j)),
            pl.BlockSpec((None, kb, tn), lambda j, i: (layer, 0, j)),
        ],
        out_specs=pl.BlockSpec((tm, tn), lambda j, i: (i, j)),
        out_shape=jax.ShapeDtypeStruct((m, n), BF16),
        scratch_shapes=[pltpu.VMEM((ka, tn), BF16), pltpu.VMEM((kb, tn), BF16)],
        compiler_params=_cparams(("arbitrary", "arbitrary")),
        name="merge",
    )(ha, hb_p, hb_s, proj_b, proj_b, wa_all, wb_all)


def _out_proj_kernel(a_ref, x_ref, w_ref, o_ref, wb_ref):
    @pl.when(pl.program_id(1) == 0)
    def _():
        wb_ref[...] = w_ref[...].astype(BF16)

    o_ref[...] = x_ref[...] + jnp.dot(a_ref[...], wb_ref[...], preferred_element_type=F32)


def _out_proj(a, x, w_all, layer, tm=1088, tn=512):
    m, k = a.shape
    n = x.shape[1]
    return pl.pallas_call(
        _out_proj_kernel,
        grid=(n // tn, m // tm),
        in_specs=[
            pl.BlockSpec((tm, k), lambda j, i: (i, 0)),
            pl.BlockSpec((tm, tn), lambda j, i: (i, j)),
            pl.BlockSpec((None, k, tn), lambda j, i: (layer, 0, j)),
        ],
        out_specs=pl.BlockSpec((tm, tn), lambda j, i: (i, j)),
        out_shape=jax.ShapeDtypeStruct((m, n), F32),
        scratch_shapes=[pltpu.VMEM((k, tn), BF16)],
        compiler_params=_cparams(("arbitrary", "arbitrary")),
        name="out_proj",
    )(a, x, w_all)


def _pool_kernel(tiles_per_seq, n_prompt_tiles, n_dec_seq,
                 xa_ref, prev_ref, za_ref, sp_ref, w_ref, sc_ref,
                 ha_ref, pp_ref, ps_ref, ext_ref, la_ref, lb_ref):
    g = pl.program_id(0)
    s = pl.program_id(1)
    tt = xa_ref.shape[0]
    wb = w_ref[...].astype(BF16)
    scale = sc_ref[...]

    def finish(pooled, za):
        y = jnp.dot(pooled.astype(BF16), wb, preferred_element_type=F32)
        return (y * scale * _silu(za)).astype(ha_ref.dtype)

    @pl.when(s < n_prompt_tiles)
    def _():
        i = s % tiles_per_seq
        xa = xa_ref[...]
        lo = 2 * POOL_PAD
        n = POOL_PAD + tt
        zeros = jnp.zeros((POOL_PAD, xa.shape[1]), F32)
        for ref in (ext_ref, la_ref, lb_ref):
            ref[0:POOL_PAD, :] = zeros
        ext_ref[POOL_PAD:lo, :] = jnp.where(i == 0, 0.0, prev_ref[...])
        ext_ref[lo:lo + tt, :] = xa
        pos = i * tt + lax.broadcasted_iota(jnp.int32, (tt, 1), 0)
        for gi, w in enumerate(POOL_WINDOWS):
            @pl.when(g == gi)
            def _(w=w):
                cur, h, level = ext_ref, 1, 0
                while h < w:
                    nxt = (la_ref, lb_ref)[level % 2]
                    nxt[POOL_PAD:POOL_PAD + n, :] = (cur[POOL_PAD:POOL_PAD + n, :]
                                                     + cur[POOL_PAD - h:POOL_PAD - h + n, :])
                    cur, h, level = nxt, 2 * h, level + 1
                inv = 1.0 / jnp.minimum(w, pos + 1).astype(F32)
                ha_ref[...] = finish(cur[lo:lo + tt, :] * inv - xa, za_ref[...])

        @pl.when(i == tiles_per_seq - 1)
        def _():
            pp_ref[0] = ext_ref[lo + tt - POOL_BUF:lo + tt, :]

    @pl.when(s == n_prompt_tiles)
    def _():
        nb = tt // n_dec_seq
        rows = [sp_ref[r] for r in range(POOL_BUF)]
        rows += [xa_ref[t * nb:(t + 1) * nb, :] for t in range(n_dec_seq)]
        for gi, w in enumerate(POOL_WINDOWS):
            @pl.when(g == gi)
            def _(w=w):
                for t in range(n_dec_seq):
                    acc = rows[POOL_BUF + t]
                    for j in range(1, w):
                        acc = acc + rows[POOL_BUF + t - j]
                    inv = 1.0 / float(min(w, PAST_LEN + t + 1))
                    pooled = acc * inv - rows[POOL_BUF + t]
                    ha_ref[t * nb:(t + 1) * nb, :] = finish(pooled, za_ref[t * nb:(t + 1) * nb, :])
        for r in range(POOL_BUF):
            ps_ref[r] = rows[r + n_dec_seq]


def _pool_mixer(proj_a, sp_t, w_pool, pool_scale3, layer, n_batch, seq, n_dec_seq, tt=512):
    m = proj_a.shape[0]
    w_a = proj_a.shape[1] // 2
    pg = w_a // len(POOL_WINDOWS)
    n_dec = sp_t.shape[2]
    assert n_dec * n_dec_seq == tt and seq % tt == 0 and m == n_batch * seq + tt
    tps = seq // tt
    npt = n_batch * tps
    ng = len(POOL_WINDOWS)
    return pl.pallas_call(
        functools.partial(_pool_kernel, tps, npt, n_dec_seq),
        grid=(ng, npt + 1),
        in_specs=[
            pl.BlockSpec((tt, pg), lambda g, s: (s, g)),
            pl.BlockSpec((POOL_PAD, pg),
                         lambda g, s: (jnp.maximum(s * (tt // POOL_PAD) - 1, 0), g)),
            pl.BlockSpec((tt, pg), lambda g, s: (s, ng + g)),
            pl.BlockSpec((None, POOL_BUF, n_dec, pg), lambda g, s: (layer, 0, 0, g)),
            pl.BlockSpec((None, None, pg, pg), lambda g, s: (layer, g, 0, 0)),
            pl.BlockSpec((None, 1, pg), lambda g, s: (layer, 0, g)),
        ],
        out_specs=[
            pl.BlockSpec((tt, pg), lambda g, s: (s, g)),
            pl.BlockSpec((1, POOL_BUF, pg), lambda g, s: (jnp.minimum(s, npt - 1) // tps, 0, g)),
            pl.BlockSpec((POOL_BUF, n_dec, pg), lambda g, s: (0, 0, g)),
        ],
        out_shape=[
            jax.ShapeDtypeStruct((m, w_a), BF16),
            jax.ShapeDtypeStruct((n_batch, POOL_BUF, w_a), F32),
            jax.ShapeDtypeStruct((POOL_BUF, n_dec, w_a), F32),
        ],
        scratch_shapes=[pltpu.VMEM((2 * POOL_PAD + tt, pg), F32)] * 3,
        compiler_params=_cparams(("arbitrary", "arbitrary")),
        name="pool_mixer",
    )(proj_a, proj_a, proj_a, sp_t, w_pool, pool_scale3)


SAMPLE_SUBSTEPS = 4
SAMPLE_SEQS = 16


def _gate_math(ig_c, ig_r, lf_r, m0_c, mask, to_row):
    b_c = jnp.sum(jnp.where(mask, lf_r, 0.0), axis=1, keepdims=True)
    dmat = jnp.where(mask, b_c - to_row(b_c) + ig_r, -jnp.inf)
    m_state = b_c + m0_c
    m_t = jnp.maximum(m_state, jnp.max(dmat, axis=1, keepdims=True))
    wts = jnp.exp(dmat - m_t)
    s_state = jnp.exp(m_state - m_t)
    return b_c, m_t, wts, s_state


def _head_gates(h, o, zb, hw):
    h = _sigmoid(o) * h
    h = h * lax.rsqrt(jnp.mean(h * h, axis=-1, keepdims=True) + EPS)
    return h * hw * _silu(zb)


def _mlstm_kernel(n_sub, c_slot, in_place, *refs):
    refs = list(refs)
    if in_place:
        del refs[10]
    (sq_ref, sk_ref, sv_ref, so_ref, sz_ref, sg_ref, sm0_ref, shw_ref, sc_in, sn_in,
     q_ref, k_ref, v_ref, o_ref, zb_ref, g_ref, hw_ref,
     hbs_ref, sc_out, sn_out, sm_out, hb_ref, c_out, n_out, m_out,
     c_ref, n_ref, m_ref, gt_ref, kt_ref, wv_ref, scl_ref, qc_ref, a_ref, p_ref) = refs
    hd = pl.program_id(1)
    c = pl.program_id(2)
    nc = pl.num_programs(2)
    L, dqk = q_ref.shape

    @pl.when(c == 0)
    def _():
        c_ref[...] = jnp.zeros_like(c_ref)
        n_ref[...] = jnp.zeros_like(n_ref)
        m_ref[...] = jnp.zeros_like(m_ref)

    T, nb, _ = sq_ref.shape
    dv = sv_ref.shape[2]
    Ls = T * nb
    per_sub = nb // n_sub
    step = (pl.program_id(0) * pl.num_programs(1) + hd) * nc + c
    sub = step % n_sub
    shd = (step // n_sub) % N_HEADS

    def flat(ref):
        return ref[...].reshape(Ls, ref.shape[2])

    sq = flat(sq_ref)

    @pl.when(sub == 0)
    def _():
        sg = flat(sg_ref)
        slane = lax.broadcasted_iota(jnp.int32, sg.shape, 1)
        sig_c = jnp.sum(jnp.where(slane == shd, sg, 0.0), axis=1, keepdims=True)
        slf_c = jnp.sum(jnp.where(slane == shd + N_HEADS, sg, 0.0), axis=1, keepdims=True)
        sm0_c = jnp.sum(jnp.where(slane == shd, flat(sm0_ref), 0.0), axis=1, keepdims=True)
        ri = lax.broadcasted_iota(jnp.int32, (Ls, Ls), 0)
        ci = lax.broadcasted_iota(jnp.int32, (Ls, Ls), 1)
        smask = jnp.logical_and(ri % nb == ci % nb, ci <= ri)
        eye = ri == ci

        def s_to_row(col):
            return jnp.sum(jnp.where(eye, col, 0.0), axis=0, keepdims=True)

        sb_c, sm_t, swts, ss_state = _gate_math(sig_c, s_to_row(sig_c), s_to_row(slf_c), sm0_c,
                                                smask, s_to_row)
        sk = flat(sk_ref) * (dqk ** -0.5)
        sv = flat(sv_ref)
        skf = sk.astype(F32)
        n_old = sn_in[...]
        sqk = lax.dot_general(sq, sk, (((1,), (1,)), ((), ())), preferred_element_type=F32) * swts
        pv = jnp.dot(sqk.astype(BF16), sv, preferred_element_type=F32)
        qn = jnp.sum(sq.astype(F32) * jnp.concatenate([n_old] * T, axis=0), axis=1, keepdims=True)
        sden = ss_state * qn + jnp.sum(sqk, axis=1, keepdims=True)
        inv = 1.0 / jnp.maximum(jnp.abs(sden), jnp.exp(-sm_t))
        a_ref[...] = ss_state * inv
        p_ref[...] = pv * inv
        qc_ref[...] = jnp.zeros_like(qc_ref)

        last = slice((T - 1) * nb, T * nb)
        sb_last = sb_c[last, :]
        sm_new = sm_t[last, :]
        scl = jnp.exp(sb_last + sm0_c[last, :] - sm_new)
        rep = lambda x: jnp.concatenate([x] * T, axis=0)
        sws = jnp.exp(rep(sb_last) - sb_c + sig_c - rep(sm_new))
        wv_ref[...] = (sws * sv.astype(F32)).astype(BF16)
        wk = sws * skf
        n_upd = wk[0:nb, :]
        for t in range(1, T):
            n_upd = n_upd + wk[t * nb:(t + 1) * nb, :]
        sn_out[...] = scl * n_old + n_upd
        sm_out[...] = sm_new
        scl_ref[...] = scl
        kt_ref[...] = skf.T.astype(BF16)

    g = g_ref[...]
    lane = lax.broadcasted_iota(jnp.int32, g.shape, 1)
    ig_c = jnp.sum(jnp.where(lane == hd, g, 0.0), axis=1, keepdims=True)
    gt_ref[...] = g.T
    ig_r = gt_ref[pl.ds(hd, 1), :]
    lf_r = gt_ref[pl.ds(hd + N_HEADS, 1), :]
    ti = lax.broadcasted_iota(jnp.int32, (L, L), 0)
    si = lax.broadcasted_iota(jnp.int32, (L, L), 1)
    m0 = m_ref[...]

    def to_row(col):
        return jnp.broadcast_to(col, (L, GATE_LANES)).T[0:1, :]

    b_c, m_t, wts, s_state = _gate_math(ig_c, ig_r, lf_r, m0, si <= ti, to_row)

    q = q_ref[...]
    k = k_ref[...] * (dqk ** -0.5)
    v = v_ref[...]
    cmat = c_ref[...]
    nvec = n_ref[...]
    s_qk = lax.dot_general(q, k, (((1,), (1,)), ((), ())), preferred_element_type=F32) * wts
    num = (s_state * jnp.dot(q, cmat.astype(BF16), preferred_element_type=F32)
           + jnp.dot(s_qk.astype(BF16), v, preferred_element_type=F32))
    den = (s_state * jnp.sum(q.astype(F32) * nvec, axis=1, keepdims=True)
           + jnp.sum(s_qk, axis=1, keepdims=True))
    hval = num / jnp.maximum(jnp.abs(den), jnp.exp(-m_t))
    hb_ref[...] = _head_gates(hval, o_ref[...].astype(F32), zb_ref[...].astype(F32),
                              hw_ref[...]).astype(hb_ref.dtype)

    b_last = b_c[L - 1:L, :]
    m_new = m_t[L - 1:L, :]
    ws = jnp.exp(b_last - b_c + ig_c - m_new)
    sc = jnp.exp(b_last + m0 - m_new)
    wv = (ws * v.astype(F32)).astype(BF16)
    c_new = sc * cmat + lax.dot_general(k, wv, (((0,), (0,)), ((), ())), preferred_element_type=F32)
    n_new = sc * nvec + jnp.sum(ws * k.astype(F32), axis=0, keepdims=True)
    c_ref[...] = c_new
    n_ref[...] = n_new
    m_ref[...] = m_new

    seq_of_row = lax.broadcasted_iota(jnp.int32, (Ls, 1), 0) % nb
    seq_of_col = lax.broadcasted_iota(jnp.int32, (1, Ls), 1) % nb
    kt = kt_ref[...]
    swv = wv_ref[...]
    qc = qc_ref[...]
    for jl in range(per_sub):
        j = sub * per_sub + jl
        cj = sc_in[jl]
        qj = jnp.where(seq_of_row == j, sq, jnp.zeros_like(sq))
        qc = qc + jnp.dot(qj, cj.astype(BF16), preferred_element_type=F32)
        ktj = jnp.where(seq_of_col == j, kt, jnp.zeros_like(kt))
        sc_out[c_slot, jl] = (scl_ref[pl.ds(j, 1), :] * cj
                              + jnp.dot(ktj, swv, preferred_element_type=F32))
    for other in range(sc_out.shape[0]):
        if other != c_slot:
            sc_out[other] = jnp.zeros(sc_out.shape[1:], F32)
    qc_ref[...] = qc

    @pl.when(c == nc - 1)
    def _():
        c_out[...] = c_new
        n_out[...] = n_new
        m_out[...] = m_new

    @pl.when(sub == n_sub - 1)
    def _():
        hs = a_ref[...] * qc + p_ref[...]
        hs = _head_gates(hs, flat(so_ref).astype(F32), flat(sz_ref).astype(F32), shw_ref[...])
        hbs_ref[...] = hs.astype(hbs_ref.dtype).reshape(T, nb, dv)


def _mlstm(proj_b, gates, m0_3, head_w3, c_all, n_t, c_acc, layer, n_batch, seq, dqk, dv, cols,
           L=256):
    in_place = c_acc is not None
    c_shape = c_acc.shape if in_place else c_all.shape
    n_dec = n_t.shape[2]
    T = m0_3.shape[0]
    H, G, nbs = N_HEADS, SAMPLE_SUBSTEPS, SAMPLE_SEQS
    nc = seq // L
    rows = n_batch * seq
    m_tot = proj_b.shape[0]
    assert m_tot == rows + T * n_dec and rows % (T * n_dec) == 0 and nc % G == 0
    assert n_batch * H * nc == (n_dec // nbs) * H * G
    q0, k0, v0, o0, z0 = (cols[n] for n in ("q", "k", "v", "o", "zb"))
    proj_b3 = proj_b.reshape(m_tot // n_dec, n_dec, proj_b.shape[1])
    gates3 = gates.reshape(m_tot // n_dec, n_dec, GATE_LANES)
    tb = rows // n_dec // T
    per_sub = nbs // G

    def rb(b, h, c):
        return b * nc + c

    def sblk(b, h, c):
        return (b * H + h) * (nc // G) + c // G

    def si(b, h, c):
        return sblk(b, h, c) // H

    def sh(b, h, c):
        return sblk(b, h, c) % H

    def cidx(b, h, c):
        return (layer, si(b, h, c) * G + c % G, sh(b, h, c), 0, 0)

    def cidx_out(b, h, c):
        return (layer if in_place else 0,) + cidx(b, h, c)[1:]

    c_out_layers = 1 if in_place else c_shape[0]
    operands = [proj_b3, proj_b3, proj_b3, proj_b3, proj_b3, gates3, m0_3, head_w3, c_all, n_t,
                proj_b, proj_b, proj_b, proj_b, proj_b, gates, head_w3]
    if in_place:
        operands.insert(10, c_acc)
    return pl.pallas_call(
        functools.partial(_mlstm_kernel, G, 0 if in_place else layer, in_place),
        grid=(n_batch, H, nc),
        in_specs=[
            pl.BlockSpec((T, nbs, dqk), lambda b, h, c: (tb, si(b, h, c), q0 // dqk + sh(b, h, c))),
            pl.BlockSpec((T, nbs, dqk), lambda b, h, c: (tb, si(b, h, c), k0 // dqk + sh(b, h, c))),
            pl.BlockSpec((T, nbs, dv), lambda b, h, c: (tb, si(b, h, c), v0 // dv + sh(b, h, c))),
            pl.BlockSpec((T, nbs, dv), lambda b, h, c: (tb, si(b, h, c), o0 // dv + sh(b, h, c))),
            pl.BlockSpec((T, nbs, dv), lambda b, h, c: (tb, si(b, h, c), z0 // dv + sh(b, h, c))),
            pl.BlockSpec((T, nbs, GATE_LANES), lambda b, h, c: (tb, si(b, h, c), 0)),
            pl.BlockSpec((T, nbs, GATE_LANES), lambda b, h, c: (0, si(b, h, c), 0)),
            pl.BlockSpec((None, 1, dv), lambda b, h, c: (layer, 0, sh(b, h, c))),
            pl.BlockSpec((None, per_sub, None, dqk, dv), cidx),
            pl.BlockSpec((None, None, nbs, dqk), lambda b, h, c: (layer, sh(b, h, c), si(b, h, c), 0)),
        ] + ([pl.BlockSpec(memory_space=pl.ANY)] if in_place else []) + [
            pl.BlockSpec((L, dqk), lambda b, h, c: (rb(b, h, c), q0 // dqk + h)),
            pl.BlockSpec((L, dqk), lambda b, h, c: (rb(b, h, c), k0 // dqk + h)),
            pl.BlockSpec((L, dv), lambda b, h, c: (rb(b, h, c), v0 // dv + h)),
            pl.BlockSpec((L, dv), lambda b, h, c: (rb(b, h, c), o0 // dv + h)),
            pl.BlockSpec((L, dv), lambda b, h, c: (rb(b, h, c), z0 // dv + h)),
            pl.BlockSpec((L, GATE_LANES), lambda b, h, c: (rb(b, h, c), 0)),
            pl.BlockSpec((None, 1, dv), lambda b, h, c: (layer, 0, h)),
        ],
        out_specs=[
            pl.BlockSpec((T, nbs, dv), lambda b, h, c: (0, si(b, h, c), sh(b, h, c))),
            pl.BlockSpec((c_out_layers, per_sub, None, dqk, dv), cidx_out),
            pl.BlockSpec((None, nbs, dqk), lambda b, h, c: (sh(b, h, c), si(b, h, c), 0)),
            pl.BlockSpec((None, nbs, 1), lambda b, h, c: (sh(b, h, c), si(b, h, c), 0)),
            pl.BlockSpec((L, dv), lambda b, h, c: (rb(b, h, c), h)),
            pl.BlockSpec((None, None, dqk, dv), lambda b, h, c: (b, h, 0, 0)),
            pl.BlockSpec((None, None, 1, dqk), lambda b, h, c: (b, h, 0, 0)),
            pl.BlockSpec((None, None, 1, 1), lambda b, h, c: (b, h, 0, 0)),
        ],
        out_shape=[
            jax.ShapeDtypeStruct((T, n_dec, H * dv), BF16),
            jax.ShapeDtypeStruct(c_shape, F32),
            jax.ShapeDtypeStruct((H, n_dec, dqk), F32),
            jax.ShapeDtypeStruct((H, n_dec, 1), F32),
            jax.ShapeDtypeStruct((rows, H * dv), BF16),
            jax.ShapeDtypeStruct((n_batch, H, dqk, dv), F32),
            jax.ShapeDtypeStruct((n_batch, H, 1, dqk), F32),
            jax.ShapeDtypeStruct((n_batch, H, 1, 1), F32),
        ],
        scratch_shapes=[
            pltpu.VMEM((dqk, dv), F32), pltpu.VMEM((1, dqk), F32), pltpu.VMEM((1, 1), F32),
            pltpu.VMEM((GATE_LANES, L), F32),
            pltpu.VMEM((dqk, T * nbs), BF16), pltpu.VMEM((T * nbs, dv), BF16),
            pltpu.VMEM((nbs, 1), F32), pltpu.VMEM((T * nbs, dv), F32),
            pltpu.VMEM((T * nbs, 1), F32), pltpu.VMEM((T * nbs, dv), F32),
        ],
        input_output_aliases={10: 1} if in_place else {},
        compiler_params=_cparams(("arbitrary", "arbitrary", "arbitrary")),
        name="mlstm",
    )(*operands)


def kernel(x_prompt, x_sample, state_pool, state_C, state_n, state_m, norm_w, w_in, b_gate, w_pool,
           pool_scale, head_norm_w, w_proj_a, w_proj_b, w_out, final_norm_w):
    n_batch, seq, d = x_prompt.shape
    n_dec, dec_seq, _ = x_sample.shape
    depth = norm_w.shape[0]
    w_a = state_pool.shape[3]
    dqk, dv = state_C.shape[3], state_C.shape[4]
    w_qk, w_b = N_HEADS * dqk, N_HEADS * dv
    n_main = 2 * w_a + 2 * w_qk + 3 * w_b + 2 * d
    assert w_in.shape[2] == n_main + 2 * N_HEADS
    n_a = 2 * w_a
    names = ("q", "k", "v", "o", "zb", "ga", "gb")
    sizes = (w_qk, w_qk, w_b, w_b, w_b, d, d)
    cols, c0 = {}, 0
    for nme, sz in zip(names, sizes):
        cols[nme] = c0
        c0 += sz
    m_p = n_batch * seq
    m_s = n_dec * dec_seq
    m = m_p + m_s

    x = jnp.concatenate([x_prompt.reshape(m_p, d),
                         x_sample.transpose(1, 0, 2).reshape(m_s, d)], axis=0)
    w_t = w_in.transpose(0, 2, 1)
    bg = jnp.pad(b_gate, ((0, 0), (0, GATE_LANES - 2 * N_HEADS)))[:, None, :]
    sp_t = state_pool.transpose(0, 2, 1, 3)
    n_t = state_n.transpose(0, 2, 1, 3)
    m0_all = jnp.pad(jnp.tile(state_m[:, None], (1, dec_seq, 1, 1)),
                     ((0, 0), (0, 0), (0, 0), (0, GATE_LANES - N_HEADS)))
    pool_scale3 = pool_scale[:, None, :]
    head_w3 = head_norm_w[:, None, :]
    norm_w3 = norm_w[:, None, :]

    pool_p, c_p, n_p, m_pr = [], [], [], []
    pool_s, n_s, m_sm = [], [], []
    c_s = None
    for l in range(depth):
        u, gates = _norm_gates(x, norm_w3[l], w_t, l, n_main, bg[l])
        proj_a = _in_proj(u, w_t, l, 0, n_a, F32, tm=1088, tn=512)
        proj_b = _in_proj(u, w_t, l, n_a, n_main - n_a, BF16)
        ha, pp, ps = _pool_mixer(proj_a, sp_t, w_pool, pool_scale3, l, n_batch, seq, dec_seq)
        hb_s, c_s, ns, ms, hb_p, cp, np_, mp = _mlstm(
            proj_b, gates, m0_all[l], head_w3, state_C, n_t, c_s, l, n_batch, seq, dqk, dv, cols)
        merged = _merge(ha, hb_p, hb_s.reshape(m_s, w_b), proj_b, cols["ga"], cols["gb"],
                        w_proj_a, w_proj_b, l)
        x = _out_proj(merged, x, w_out, l)
        pool_p.append(pp); c_p.append(cp); n_p.append(np_.reshape(n_batch, N_HEADS, dqk))
        m_pr.append(mp.reshape(n_batch, N_HEADS))
        pool_s.append(ps.transpose(1, 0, 2)); n_s.append(ns.transpose(1, 0, 2))
        m_sm.append(ms.reshape(N_HEADS, n_dec).T)

    y_p, y_s = _final_norm(x, final_norm_w[None, :], m_p)
    y_prompt = y_p.reshape(n_batch, seq, d)
    y_sample = y_s.reshape(dec_seq, n_dec, d).transpose(1, 0, 2)
    return (y_prompt, y_sample,
            jnp.stack(pool_p), jnp.stack(c_p), jnp.stack(n_p), jnp.stack(m_pr),
            jnp.stack(pool_s), c_s, jnp.stack(n_s), jnp.stack(m_sm))
```

```python
import functools

import jax
import jax.numpy as jnp
from jax import lax
from jax.experimental import pallas as pl
from jax.experimental.pallas import tpu as pltpu

F32 = jnp.float32
BF16 = jnp.bfloat16

POOL_WINDOWS = (2, 4, 8, 16)
POOL_BUF = max(POOL_WINDOWS) - 1
POOL_PAD = 16
ZERO_TILE_HEADS = 1
N_HEADS = 8
EPS = 1e-6
PAST_LEN = 16384
GATE_LANES = 128
VMEM_LIMIT = 56 * 1024 * 1024


def _cparams(sem):
    return pltpu.CompilerParams(dimension_semantics=sem, vmem_limit_bytes=VMEM_LIMIT)


def _log_sigmoid(x):
    return jnp.minimum(x, 0.0) - jnp.log1p(jnp.exp(-jnp.abs(x)))


def _sigmoid(x):
    return 0.5 * jnp.tanh(0.5 * x) + 0.5


def _silu(x):
    return x * _sigmoid(x)


def _norm_gate_kernel(x_ref, nw_ref, wg_ref, bg_ref, u_ref, g_ref):
    x = x_ref[...]
    r = lax.rsqrt(jnp.mean(x * x, axis=-1, keepdims=True) + EPS)
    ub = (x * r * nw_ref[...]).astype(BF16)
    u_ref[...] = ub
    wg = wg_ref[...].astype(BF16)
    wg = jnp.concatenate([wg, jnp.zeros((GATE_LANES - wg.shape[0], wg.shape[1]), BF16)], axis=0)
    g = lax.dot_general(ub, wg, (((1,), (1,)), ((), ())), preferred_element_type=F32) + bg_ref[...]
    lane = lax.broadcasted_iota(jnp.int32, g.shape, 1)
    g_ref[...] = jnp.where(lane >= N_HEADS, _log_sigmoid(g), g)


def _norm_gates(x, nw, w_t, layer, gate_row0, bg, tm=512):
    m, d = x.shape
    ng = 2 * N_HEADS
    assert gate_row0 % ng == 0 and w_t.shape[1] - gate_row0 == ng
    return pl.pallas_call(
        _norm_gate_kernel,
        grid=(m // tm,),
        in_specs=[
            pl.BlockSpec((tm, d), lambda i: (i, 0)),
            pl.BlockSpec((1, d), lambda i: (0, 0)),
            pl.BlockSpec((None, ng, d), lambda i: (layer, gate_row0 // ng, 0)),
            pl.BlockSpec((1, GATE_LANES), lambda i: (0, 0)),
        ],
        out_specs=[
            pl.BlockSpec((tm, d), lambda i: (i, 0)),
            pl.BlockSpec((tm, GATE_LANES), lambda i: (i, 0)),
        ],
        out_shape=[
            jax.ShapeDtypeStruct((m, d), BF16),
            jax.ShapeDtypeStruct((m, GATE_LANES), F32),
        ],
        compiler_params=_cparams(("arbitrary",)),
        name="norm_gates",
    )(x, nw, w_t, bg)


def _final_norm_kernel(n_prompt_blocks, x_ref, nw_ref, yp_ref, ys_ref):
    i = pl.program_id(0)
    x = x_ref[...]
    r = lax.rsqrt(jnp.mean(x * x, axis=-1, keepdims=True) + EPS)
    y = x * r * nw_ref[...]

    @pl.when(i < n_prompt_blocks)
    def _():
        yp_ref[...] = y

    @pl.when(i == n_prompt_blocks)
    def _():
        ys_ref[...] = y


def _final_norm(x, nw, m_p):
    m, d = x.shape
    tm = m - m_p
    assert m_p % tm == 0
    npb = m_p // tm
    return pl.pallas_call(
        functools.partial(_final_norm_kernel, npb),
        grid=(npb + 1,),
        in_specs=[pl.BlockSpec((tm, d), lambda i: (i, 0)),
                  pl.BlockSpec((1, d), lambda i: (0, 0))],
        out_specs=[pl.BlockSpec((tm, d), lambda i: (jnp.minimum(i, npb - 1), 0)),
                   pl.BlockSpec((tm, d), lambda i: (0, 0))],
        out_shape=[jax.ShapeDtypeStruct((m_p, d), F32), jax.ShapeDtypeStruct((tm, d), F32)],
        compiler_params=_cparams(("arbitrary",)),
        name="final_norm",
    )(x, nw)


def _in_proj_kernel(fill, u_ref, w_ref, o_ref, *rest):
    i = pl.program_id(1)
    if fill:
        zbuf_hbm, wb_ref, zeros_ref, sem = rest
        units, n_steps = fill
        tile = zeros_ref.shape[0]
        _, d1, d2 = zbuf_hbm.shape[:3]
        per_d2 = d2 // tile
        step = pl.program_id(0) * pl.num_programs(1) + i
        lo = step * units // n_steps
        hi = (step + 1) * units // n_steps

        def fill_copy(unit):
            dst = zbuf_hbm.at[unit // (d1 * per_d2), (unit // per_d2) % d1,
                              pl.ds((unit % per_d2) * tile, tile)]
            return pltpu.make_async_copy(zeros_ref, dst, sem.at[0])

        def for_own_units(fn):
            for t in range(-(-units // n_steps)):
                pl.when(lo + t < hi)(functools.partial(fn, lo + t))

        @pl.when(step == 0)
        def _():
            zeros_ref[...] = jnp.zeros_like(zeros_ref)

        for_own_units(lambda unit: fill_copy(unit).start())
    else:
        (wb_ref,) = rest

    @pl.when(i == 0)
    def _():
        wb_ref[...] = w_ref[...].astype(BF16)

    o_ref[...] = lax.dot_general(u_ref[...], wb_ref[...], (((1,), (1,)), ((), ())),
                                 preferred_element_type=F32).astype(o_ref.dtype)
    if fill:
        for_own_units(lambda unit: fill_copy(unit).wait())


def _in_proj(u, w_t, layer, col0, ncols, out_dtype, tm=544, tn=1024, zero_buffer_shape=None):
    m, k = u.shape
    off = col0 // tn
    grid = (ncols // tn, m // tm)
    out_specs = [pl.BlockSpec((tm, tn), lambda j, i: (i, j))]
    out_shape = [jax.ShapeDtypeStruct((m, ncols), out_dtype)]
    scratch = [pltpu.VMEM((tn, k), BF16)]
    fill = None
    if zero_buffer_shape is not None:
        a, b, h, r, c = zero_buffer_shape
        assert h % ZERO_TILE_HEADS == 0
        fill = (a * b * (h // ZERO_TILE_HEADS), grid[0] * grid[1])
        out_specs.append(pl.BlockSpec(memory_space=pl.ANY))
        out_shape.append(jax.ShapeDtypeStruct(zero_buffer_shape, F32))
        scratch += [pltpu.VMEM((ZERO_TILE_HEADS, r, c), F32), pltpu.SemaphoreType.DMA((1,))]
    outs = pl.pallas_call(
        functools.partial(_in_proj_kernel, fill),
        grid=grid,
        in_specs=[
            pl.BlockSpec((tm, k), lambda j, i: (i, 0)),
            pl.BlockSpec((None, tn, k), lambda j, i: (layer, j + off, 0)),
        ],
        out_specs=out_specs,
        out_shape=out_shape,
        scratch_shapes=scratch,
        compiler_params=_cparams(("arbitrary", "arbitrary")),
        name="in_proj",
    )(u, w_t)
    return outs[0] if fill is None else outs


def _merge_kernel(n_prompt_blocks, ha_ref, hbp_ref, hbs_ref, ga_ref, gb_ref, wa_ref, wb_ref,
                  o_ref, wab_ref, wbb_ref):
    i = pl.program_id(1)

    @pl.when(i == 0)
    def _():
        wab_ref[...] = wa_ref[...].astype(BF16)
        wbb_ref[...] = wb_ref[...].astype(BF16)

    hb = jnp.where(i >= n_prompt_blocks, hbs_ref[...], hbp_ref[...])
    pa = jnp.dot(ha_ref[...], wab_ref[...], preferred_element_type=F32)
    pb = jnp.dot(hb, wbb_ref[...], preferred_element_type=F32)
    ga = _sigmoid(ga_ref[...].astype(F32))
    gb = _sigmoid(gb_ref[...].astype(F32))
    o_ref[...] = (ga * pa + gb * pb).astype(o_ref.dtype)


def _merge(ha, hb_p, hb_s, proj_b, ga_col0, gb_col0, wa_all, wb_all, layer, tn=512):
    m, ka = ha.shape
    kb = hb_p.shape[1]
    n = wa_all.shape[2]
    tm = hb_s.shape[0]
    npb = hb_p.shape[0] // tm
    return pl.pallas_call(
        functools.partial(_merge_kernel, npb),
        grid=(n // tn, m // tm),
        in_specs=[
            pl.BlockSpec((tm, ka), lambda j, i: (i, 0)),
            pl.BlockSpec((tm, kb), lambda j, i: (jnp.minimum(i, npb - 1), 0)),
            pl.BlockSpec((tm, kb), lambda j, i: (0, 0)),
            pl.BlockSpec((tm, tn), lambda j, i: (i, j + ga_col0 // tn)),
            pl.BlockSpec((tm, tn), lambda j, i: (i, j + gb_col0 // tn)),
            pl.BlockSpec((None, ka, tn), lambda j, i: (layer, 0, j)),
            pl.BlockSpec((None, kb, tn), lambda j, i: (layer, 0, j)),
        ],
        out_specs=pl.BlockSpec((tm, tn), lambda j, i: (i, j)),
        out_shape=jax.ShapeDtypeStruct((m, n), BF16),
        scratch_shapes=[pltpu.VMEM((ka, tn), BF16), pltpu.VMEM((kb, tn), BF16)],
        compiler_params=_cparams(("arbitrary", "arbitrary")),
        name="merge",
    )(ha, hb_p, hb_s, proj_b, proj_b, wa_all, wb_all)


def _out_proj_kernel(a_ref, x_ref, w_ref, o_ref, wb_ref):
    @pl.when(pl.program_id(1) == 0)
    def _():
        wb_ref[...] = w_ref[...].astype(BF16)

    o_ref[...] = x_ref[...] + jnp.dot(a_ref[...], wb_ref[...], preferred_element_type=F32)


def _out_proj(a, x, w_all, layer, tm=1088, tn=512):
    m, k = a.shape
    n = x.shape[1]
    return pl.pallas_call(
        _out_proj_kernel,
        grid=(n // tn, m // tm),
        in_specs=[
            pl.BlockSpec((tm, k), lambda j, i: (i, 0)),
            pl.BlockSpec((tm, tn), lambda j, i: (i, j)),
            pl.BlockSpec((None, k, tn), lambda j, i: (layer, 0, j)),
        ],
        out_specs=pl.BlockSpec((tm, tn), lambda j, i: (i, j)),
        out_shape=jax.ShapeDtypeStruct((m, n), F32),
        scratch_shapes=[pltpu.VMEM((k, tn), BF16)],
        compiler_params=_cparams(("arbitrary", "arbitrary")),
        name="out_proj",
    )(a, x, w_all)


def _pool_kernel(tiles_per_seq, n_prompt_tiles, n_dec_seq,
                 xa_ref, prev_ref, za_ref, sp_ref, w_ref, sc_ref,
                 ha_ref, pp_ref, ps_ref, ext_ref, la_ref, lb_ref):
    g = pl.program_id(0)
    s = pl.program_id(1)
    tt = xa_ref.shape[0]
    wb = w_ref[...].astype(BF16)
    scale = sc_ref[...]

    def finish(pooled, za):
        y = jnp.dot(pooled.astype(BF16), wb, preferred_element_type=F32)
        return (y * scale * _silu(za)).astype(ha_ref.dtype)

    @pl.when(s < n_prompt_tiles)
    def _():
        i = s % tiles_per_seq
        xa = xa_ref[...]
        lo = 2 * POOL_PAD
        n = POOL_PAD + tt
        zeros = jnp.zeros((POOL_PAD, xa.shape[1]), F32)
        for ref in (ext_ref, la_ref, lb_ref):
            ref[0:POOL_PAD, :] = zeros
        ext_ref[POOL_PAD:lo, :] = jnp.where(i == 0, 0.0, prev_ref[...])
        ext_ref[lo:lo + tt, :] = xa
        pos = i * tt + lax.broadcasted_iota(jnp.int32, (tt, 1), 0)
        for gi, w in enumerate(POOL_WINDOWS):
            @pl.when(g == gi)
            def _(w=w):
                cur, h, level = ext_ref, 1, 0
                while h < w:
                    nxt = (la_ref, lb_ref)[level % 2]
                    nxt[POOL_PAD:POOL_PAD + n, :] = (cur[POOL_PAD:POOL_PAD + n, :]
                                                     + cur[POOL_PAD - h:POOL_PAD - h + n, :])
                    cur, h, level = nxt, 2 * h, level + 1
                inv = 1.0 / jnp.minimum(w, pos + 1).astype(F32)
                ha_ref[...] = finish(cur[lo:lo + tt, :] * inv - xa, za_ref[...])

        @pl.when(i == tiles_per_seq - 1)
        def _():
            pp_ref[0] = ext_ref[lo + tt - POOL_BUF:lo + tt, :]

    @pl.when(s == n_prompt_tiles)
    def _():
        nb = tt // n_dec_seq
        rows = [sp_ref[r] for r in range(POOL_BUF)]
        rows += [xa_ref[t * nb:(t + 1) * nb, :] for t in range(n_dec_seq)]
        for gi, w in enumerate(POOL_WINDOWS):
            @pl.when(g == gi)
            def _(w=w):
                for t in range(n_dec_seq):
                    acc = rows[POOL_BUF + t]
                    for j in range(1, w):
                        acc = acc + rows[POOL_BUF + t - j]
                    inv = 1.0 / float(min(w, PAST_LEN + t + 1))
                    pooled = acc * inv - rows[POOL_BUF + t]
                    ha_ref[t * nb:(t + 1) * nb, :] = finish(pooled, za_ref[t * nb:(t + 1) * nb, :])
        for r in range(POOL_BUF):
            ps_ref[r] = rows[r + n_dec_seq]


def _pool_mixer(proj_a, sp_t, w_pool, pool_scale3, layer, n_batch, seq, n_dec_seq, tt=512):
    m = proj_a.shape[0]
    w_a = proj_a.shape[1] // 2
    pg = w_a // len(POOL_WINDOWS)
    n_dec = sp_t.shape[2]
    assert n_dec * n_dec_seq == tt and seq % tt == 0 and m == n_batch * seq + tt
    tps = seq // tt
    npt = n_batch * tps
    ng = len(POOL_WINDOWS)
    return pl.pallas_call(
        functools.partial(_pool_kernel, tps, npt, n_dec_seq),
        grid=(ng, npt + 1),
        in_specs=[
            pl.BlockSpec((tt, pg), lambda g, s: (s, g)),
            pl.BlockSpec((POOL_PAD, pg),
                         lambda g, s: (jnp.maximum(s * (tt // POOL_PAD) - 1, 0), g)),
            pl.BlockSpec((tt, pg), lambda g, s: (s, ng + g)),
            pl.BlockSpec((None, POOL_BUF, n_dec, pg), lambda g, s: (layer, 0, 0, g)),
            pl.BlockSpec((None, None, pg, pg), lambda g, s: (layer, g, 0, 0)),
            pl.BlockSpec((None, 1, pg), lambda g, s: (layer, 0, g)),
        ],
        out_specs=[
            pl.BlockSpec((tt, pg), lambda g, s: (s, g)),
            pl.BlockSpec((1, POOL_BUF, pg), lambda g, s: (jnp.minimum(s, npt - 1) // tps, 0, g)),
            pl.BlockSpec((POOL_BUF, n_dec, pg), lambda g, s: (0, 0, g)),
        ],
        out_shape=[
            jax.ShapeDtypeStruct((m, w_a), BF16),
            jax.ShapeDtypeStruct((n_batch, POOL_BUF, w_a), F32),
            jax.ShapeDtypeStruct((POOL_BUF, n_dec, w_a), F32),
        ],
        scratch_shapes=[pltpu.VMEM((2 * POOL_PAD + tt, pg), F32)] * 3,
        compiler_params=_cparams(("arbitrary", "arbitrary")),
        name="pool_mixer",
    )(proj_a, proj_a, proj_a, sp_t, w_pool, pool_scale3)


SAMPLE_SUBSTEPS = 4
SAMPLE_SEQS = 16


def _gate_math(ig_c, ig_r, lf_r, m0_c, mask, to_row):
    b_c = jnp.sum(jnp.where(mask, lf_r, 0.0), axis=1, keepdims=True)
    dmat = jnp.where(mask, b_c - to_row(b_c) + ig_r, -jnp.inf)
    m_state = b_c + m0_c
    m_t = jnp.maximum(m_state, jnp.max(dmat, axis=1, keepdims=True))
    wts = jnp.exp(dmat - m_t)
    s_state = jnp.exp(m_state - m_t)
    return b_c, m_t, wts, s_state


def _head_gates(h, o, zb, hw):
    h = _sigmoid(o) * h
    h = h * lax.rsqrt(jnp.mean(h * h, axis=-1, keepdims=True) + EPS)
    return h * hw * _silu(zb)


def _mlstm_kernel(n_sub, sq_ref, sk_ref, sv_ref, so_ref, sz_ref, sg_ref, sm0_ref, shw_ref, sc_in,
                  sn_in, c_acc_hbm,
                  q_ref, k_ref, v_ref, o_ref, zb_ref, g_ref, hw_ref,
                  hbs_ref, sc_out, sn_out, sm_out, hb_ref, c_out, n_out, m_out,
                  c_ref, n_ref, m_ref, gt_ref, kt_ref, wv_ref, scl_ref, qc_ref, a_ref, p_ref):
    del c_acc_hbm
    hd = pl.program_id(1)
    c = pl.program_id(2)
    nc = pl.num_programs(2)
    L, dqk = q_ref.shape

    @pl.when(c == 0)
    def _():
        c_ref[...] = jnp.zeros_like(c_ref)
        n_ref[...] = jnp.zeros_like(n_ref)
        m_ref[...] = jnp.zeros_like(m_ref)

    T, nb, _ = sq_ref.shape
    dv = sv_ref.shape[2]
    Ls = T * nb
    per_sub = nb // n_sub
    step = (pl.program_id(0) * pl.num_programs(1) + hd) * nc + c
    sub = step % n_sub
    shd = (step // n_sub) % N_HEADS

    def flat(ref):
        return ref[...].reshape(Ls, ref.shape[2])

    sq = flat(sq_ref)

    @pl.when(sub == 0)
    def _():
        sg = flat(sg_ref)
        slane = lax.broadcasted_iota(jnp.int32, sg.shape, 1)
        sig_c = jnp.sum(jnp.where(slane == shd, sg, 0.0), axis=1, keepdims=True)
        slf_c = jnp.sum(jnp.where(slane == shd + N_HEADS, sg, 0.0), axis=1, keepdims=True)
        sm0_c = jnp.sum(jnp.where(slane == shd, flat(sm0_ref), 0.0), axis=1, keepdims=True)
        ri = lax.broadcasted_iota(jnp.int32, (Ls, Ls), 0)
        ci = lax.broadcasted_iota(jnp.int32, (Ls, Ls), 1)
        smask = jnp.logical_and(ri % nb == ci % nb, ci <= ri)
        eye = ri == ci

        def s_to_row(col):
            return jnp.sum(jnp.where(eye, col, 0.0), axis=0, keepdims=True)

        sb_c, sm_t, swts, ss_state = _gate_math(sig_c, s_to_row(sig_c), s_to_row(slf_c), sm0_c,
                                                smask, s_to_row)
        sk = flat(sk_ref) * (dqk ** -0.5)
        sv = flat(sv_ref)
        skf = sk.astype(F32)
        n_old = sn_in[...]
        sqk = lax.dot_general(sq, sk, (((1,), (1,)), ((), ())), preferred_element_type=F32) * swts
        pv = jnp.dot(sqk.astype(BF16), sv, preferred_element_type=F32)
        qn = jnp.sum(sq.astype(F32) * jnp.concatenate([n_old] * T, axis=0), axis=1, keepdims=True)
        sden = ss_state * qn + jnp.sum(sqk, axis=1, keepdims=True)
        inv = 1.0 / jnp.maximum(jnp.abs(sden), jnp.exp(-sm_t))
        a_ref[...] = ss_state * inv
        p_ref[...] = pv * inv
        qc_ref[...] = jnp.zeros_like(qc_ref)

        last = slice((T - 1) * nb, T * nb)
        sb_last = sb_c[last, :]
        sm_new = sm_t[last, :]
        scl = jnp.exp(sb_last + sm0_c[last, :] - sm_new)
        rep = lambda x: jnp.concatenate([x] * T, axis=0)
        sws = jnp.exp(rep(sb_last) - sb_c + sig_c - rep(sm_new))
        wv_ref[...] = (sws * sv.astype(F32)).astype(BF16)
        wk = sws * skf
        n_upd = wk[0:nb, :]
        for t in range(1, T):
            n_upd = n_upd + wk[t * nb:(t + 1) * nb, :]
        sn_out[...] = scl * n_old + n_upd
        sm_out[...] = sm_new
        scl_ref[...] = scl
        kt_ref[...] = skf.T.astype(BF16)

    g = g_ref[...]
    lane = lax.broadcasted_iota(jnp.int32, g.shape, 1)
    ig_c = jnp.sum(jnp.where(lane == hd, g, 0.0), axis=1, keepdims=True)
    gt_ref[...] = g.T
    ig_r = gt_ref[pl.ds(hd, 1), :]
    lf_r = gt_ref[pl.ds(hd + N_HEADS, 1), :]
    ti = lax.broadcasted_iota(jnp.int32, (L, L), 0)
    si = lax.broadcasted_iota(jnp.int32, (L, L), 1)
    m0 = m_ref[...]

    def to_row(col):
        return jnp.broadcast_to(col, (L, GATE_LANES)).T[0:1, :]

    b_c, m_t, wts, s_state = _gate_math(ig_c, ig_r, lf_r, m0, si <= ti, to_row)

    q = q_ref[...]
    k = k_ref[...] * (dqk ** -0.5)
    v = v_ref[...]
    cmat = c_ref[...]
    nvec = n_ref[...]
    s_qk = lax.dot_general(q, k, (((1,), (1,)), ((), ())), preferred_element_type=F32) * wts
    num = (s_state * jnp.dot(q, cmat.astype(BF16), preferred_element_type=F32)
           + jnp.dot(s_qk.astype(BF16), v, preferred_element_type=F32))
    den = (s_state * jnp.sum(q.astype(F32) * nvec, axis=1, keepdims=True)
           + jnp.sum(s_qk, axis=1, keepdims=True))
    hval = num / jnp.maximum(jnp.abs(den), jnp.exp(-m_t))
    hb_ref[...] = _head_gates(hval, o_ref[...].astype(F32), zb_ref[...].astype(F32),
                              hw_ref[...]).astype(hb_ref.dtype)

    b_last = b_c[L - 1:L, :]
    m_new = m_t[L - 1:L, :]
    ws = jnp.exp(b_last - b_c + ig_c - m_new)
    sc = jnp.exp(b_last + m0 - m_new)
    wv = (ws * v.astype(F32)).astype(BF16)
    c_new = sc * cmat + lax.dot_general(k, wv, (((0,), (0,)), ((), ())), preferred_element_type=F32)
    n_new = sc * nvec + jnp.sum(ws * k.astype(F32), axis=0, keepdims=True)
    c_ref[...] = c_new
    n_ref[...] = n_new
    m_ref[...] = m_new

    seq_of_row = lax.broadcasted_iota(jnp.int32, (Ls, 1), 0) % nb
    seq_of_col = lax.broadcasted_iota(jnp.int32, (1, Ls), 1) % nb
    kt = kt_ref[...]
    swv = wv_ref[...]
    qc = qc_ref[...]
    for jl in range(per_sub):
        j = sub * per_sub + jl
        cj = sc_in[jl]
        qj = jnp.where(seq_of_row == j, sq, jnp.zeros_like(sq))
        qc = qc + jnp.dot(qj, cj.astype(BF16), preferred_element_type=F32)
        ktj = jnp.where(seq_of_col == j, kt, jnp.zeros_like(kt))
        sc_out[jl] = scl_ref[pl.ds(j, 1), :] * cj + jnp.dot(ktj, swv, preferred_element_type=F32)
    qc_ref[...] = qc

    @pl.when(c == nc - 1)
    def _():
        c_out[...] = c_new
        n_out[...] = n_new
        m_out[...] = m_new

    @pl.when(sub == n_sub - 1)
    def _():
        hs = a_ref[...] * qc + p_ref[...]
        hs = _head_gates(hs, flat(so_ref).astype(F32), flat(sz_ref).astype(F32), shw_ref[...])
        hbs_ref[...] = hs.astype(hbs_ref.dtype).reshape(T, nb, dv)


def _mlstm(proj_b, gates, m0_3, head_w3, c_all, n_t, c_acc, layer, n_batch, seq, dqk, dv, cols,
           L=256):
    n_dec = n_t.shape[2]
    T = m0_3.shape[0]
    H, G, nbs = N_HEADS, SAMPLE_SUBSTEPS, SAMPLE_SEQS
    nc = seq // L
    rows = n_batch * seq
    m_tot = proj_b.shape[0]
    assert m_tot == rows + T * n_dec and rows % (T * n_dec) == 0 and nc % G == 0
    assert n_batch * H * nc == (n_dec // nbs) * H * G
    q0, k0, v0, o0, z0 = (cols[n] for n in ("q", "k", "v", "o", "zb"))
    proj_b3 = proj_b.reshape(m_tot // n_dec, n_dec, proj_b.shape[1])
    gates3 = gates.reshape(m_tot // n_dec, n_dec, GATE_LANES)
    tb = rows // n_dec // T
    per_sub = nbs // G

    def rb(b, h, c):
        return b * nc + c

    def sblk(b, h, c):
        return (b * H + h) * (nc // G) + c // G

    def si(b, h, c):
        return sblk(b, h, c) // H

    def sh(b, h, c):
        return sblk(b, h, c) % H

    def cidx(b, h, c):
        return (layer, si(b, h, c) * G + c % G, sh(b, h, c), 0, 0)

    return pl.pallas_call(
        functools.partial(_mlstm_kernel, G),
        grid=(n_batch, H, nc),
        in_specs=[
            pl.BlockSpec((T, nbs, dqk), lambda b, h, c: (tb, si(b, h, c), q0 // dqk + sh(b, h, c))),
            pl.BlockSpec((T, nbs, dqk), lambda b, h, c: (tb, si(b, h, c), k0 // dqk + sh(b, h, c))),
            pl.BlockSpec((T, nbs, dv), lambda b, h, c: (tb, si(b, h, c), v0 // dv + sh(b, h, c))),
            pl.BlockSpec((T, nbs, dv), lambda b, h, c: (tb, si(b, h, c), o0 // dv + sh(b, h, c))),
            pl.BlockSpec((T, nbs, dv), lambda b, h, c: (tb, si(b, h, c), z0 // dv + sh(b, h, c))),
            pl.BlockSpec((T, nbs, GATE_LANES), lambda b, h, c: (tb, si(b, h, c), 0)),
            pl.BlockSpec((T, nbs, GATE_LANES), lambda b, h, c: (0, si(b, h, c), 0)),
            pl.BlockSpec((None, 1, dv), lambda b, h, c: (layer, 0, sh(b, h, c))),
            pl.BlockSpec((None, per_sub, None, dqk, dv), cidx),
            pl.BlockSpec((None, None, nbs, dqk), lambda b, h, c: (layer, sh(b, h, c), si(b, h, c), 0)),
            pl.BlockSpec(memory_space=pl.ANY),
            pl.BlockSpec((L, dqk), lambda b, h, c: (rb(b, h, c), q0 // dqk + h)),
            pl.BlockSpec((L, dqk), lambda b, h, c: (rb(b, h, c), k0 // dqk + h)),
            pl.BlockSpec((L, dv), lambda b, h, c: (rb(b, h, c), v0 // dv + h)),
            pl.BlockSpec((L, dv), lambda b, h, c: (rb(b, h, c), o0 // dv + h)),
            pl.BlockSpec((L, dv), lambda b, h, c: (rb(b, h, c), z0 // dv + h)),
            pl.BlockSpec((L, GATE_LANES), lambda b, h, c: (rb(b, h, c), 0)),
            pl.BlockSpec((None, 1, dv), lambda b, h, c: (layer, 0, h)),
        ],
        out_specs=[
            pl.BlockSpec((T, nbs, dv), lambda b, h, c: (0, si(b, h, c), sh(b, h, c))),
            pl.BlockSpec((None, per_sub, None, dqk, dv), cidx),
            pl.BlockSpec((None, nbs, dqk), lambda b, h, c: (sh(b, h, c), si(b, h, c), 0)),
            pl.BlockSpec((None, nbs, 1), lambda b, h, c: (sh(b, h, c), si(b, h, c), 0)),
            pl.BlockSpec((L, dv), lambda b, h, c: (rb(b, h, c), h)),
            pl.BlockSpec((None, None, dqk, dv), lambda b, h, c: (b, h, 0, 0)),
            pl.BlockSpec((None, None, 1, dqk), lambda b, h, c: (b, h, 0, 0)),
            pl.BlockSpec((None, None, 1, 1), lambda b, h, c: (b, h, 0, 0)),
        ],
        out_shape=[
            jax.ShapeDtypeStruct((T, n_dec, H * dv), BF16),
            jax.ShapeDtypeStruct(c_acc.shape, F32),
            jax.ShapeDtypeStruct((H, n_dec, dqk), F32),
            jax.ShapeDtypeStruct((H, n_dec, 1), F32),
            jax.ShapeDtypeStruct((rows, H * dv), BF16),
            jax.ShapeDtypeStruct((n_batch, H, dqk, dv), F32),
            jax.ShapeDtypeStruct((n_batch, H, 1, dqk), F32),
            jax.ShapeDtypeStruct((n_batch, H, 1, 1), F32),
        ],
        scratch_shapes=[
            pltpu.VMEM((dqk, dv), F32), pltpu.VMEM((1, dqk), F32), pltpu.VMEM((1, 1), F32),
            pltpu.VMEM((GATE_LANES, L), F32),
            pltpu.VMEM((dqk, T * nbs), BF16), pltpu.VMEM((T * nbs, dv), BF16),
            pltpu.VMEM((nbs, 1), F32), pltpu.VMEM((T * nbs, dv), F32),
            pltpu.VMEM((T * nbs, 1), F32), pltpu.VMEM((T * nbs, dv), F32),
        ],
        input_output_aliases={10: 1},
        compiler_params=_cparams(("arbitrary", "arbitrary", "arbitrary")),
        name="mlstm",
    )(proj_b3, proj_b3, proj_b3, proj_b3, proj_b3, gates3, m0_3, head_w3, c_all, n_t, c_acc,
      proj_b, proj_b, proj_b, proj_b, proj_b, gates, head_w3)


def kernel(x_prompt, x_sample, state_pool, state_C, state_n, state_m, norm_w, w_in, b_gate, w_pool,
           pool_scale, head_norm_w, w_proj_a, w_proj_b, w_out, final_norm_w):
    n_batch, seq, d = x_prompt.shape
    n_dec, dec_seq, _ = x_sample.shape
    depth = norm_w.shape[0]
    w_a = state_pool.shape[3]
    dqk, dv = state_C.shape[3], state_C.shape[4]
    w_qk, w_b = N_HEADS * dqk, N_HEADS * dv
    n_main = 2 * w_a + 2 * w_qk + 3 * w_b + 2 * d
    assert w_in.shape[2] == n_main + 2 * N_HEADS
    n_a = 2 * w_a
    names = ("q", "k", "v", "o", "zb", "ga", "gb")
    sizes = (w_qk, w_qk, w_b, w_b, w_b, d, d)
    cols, c0 = {}, 0
    for nme, sz in zip(names, sizes):
        cols[nme] = c0
        c0 += sz
    m_p = n_batch * seq
    m_s = n_dec * dec_seq
    m = m_p + m_s

    x = jnp.concatenate([x_prompt.reshape(m_p, d),
                         x_sample.transpose(1, 0, 2).reshape(m_s, d)], axis=0)
    w_t = w_in.transpose(0, 2, 1)
    bg = jnp.pad(b_gate, ((0, 0), (0, GATE_LANES - 2 * N_HEADS)))[:, None, :]
    sp_t = state_pool.transpose(0, 2, 1, 3)
    n_t = state_n.transpose(0, 2, 1, 3)
    m0_all = jnp.pad(jnp.tile(state_m[:, None], (1, dec_seq, 1, 1)),
                     ((0, 0), (0, 0), (0, 0), (0, GATE_LANES - N_HEADS)))
    pool_scale3 = pool_scale[:, None, :]
    head_w3 = head_norm_w[:, None, :]
    norm_w3 = norm_w[:, None, :]

    pool_p, c_p, n_p, m_pr = [], [], [], []
    pool_s, n_s, m_sm = [], [], []
    for l in range(depth):
        u, gates = _norm_gates(x, norm_w3[l], w_t, l, n_main, bg[l])
        proj_a = _in_proj(u, w_t, l, 0, n_a, F32, tm=1088, tn=512)
        if l == 0:
            proj_b, c_s = _in_proj(u, w_t, l, n_a, n_main - n_a, BF16,
                                   zero_buffer_shape=state_C.shape)
        else:
            proj_b = _in_proj(u, w_t, l, n_a, n_main - n_a, BF16)
        ha, pp, ps = _pool_mixer(proj_a, sp_t, w_pool, pool_scale3, l, n_batch, seq, dec_seq)
        hb_s, c_s, ns, ms, hb_p, cp, np_, mp = _mlstm(
            proj_b, gates, m0_all[l], head_w3, state_C, n_t, c_s, l, n_batch, seq, dqk, dv, cols)
        merged = _merge(ha, hb_p, hb_s.reshape(m_s, w_b), proj_b, cols["ga"], cols["gb"],
                        w_proj_a, w_proj_b, l)
        x = _out_proj(merged, x, w_out, l)
        pool_p.append(pp); c_p.append(cp); n_p.append(np_.reshape(n_batch, N_HEADS, dqk))
        m_pr.append(mp.reshape(n_batch, N_HEADS))
        pool_s.append(ps.transpose(1, 0, 2)); n_s.append(ns.transpose(1, 0, 2))
        m_sm.append(ms.reshape(N_HEADS, n_dec).T)

    y_p, y_s = _final_norm(x, final_norm_w[None, :], m_p)
    y_prompt = y_p.reshape(n_batch, seq, d)
    y_sample = y_s.reshape(dec_seq, n_dec, d).transpose(1, 0, 2)
    return (y_prompt, y_sample,
            jnp.stack(pool_p), jnp.stack(c_p), jnp.stack(n_p), jnp.stack(m_pr),
            jnp.stack(pool_s), c_s, jnp.stack(n_s), jnp.stack(m_sm))
```

```python
import functools

import jax
import jax.numpy as jnp
from jax import lax
from jax.experimental import pallas as pl
from jax.experimental.pallas import tpu as pltpu

F32 = jnp.float32
BF16 = jnp.bfloat16

POOL_WINDOWS = (2, 4, 8, 16)
POOL_BUF = max(POOL_WINDOWS) - 1
POOL_PAD = 16
ZERO_TILE_HEADS = 1
N_HEADS = 8
EPS = 1e-6
PAST_LEN = 16384
GATE_LANES = 128
VMEM_LIMIT = 56 * 1024 * 1024


def _cparams(sem):
    return pltpu.CompilerParams(dimension_semantics=sem, vmem_limit_bytes=VMEM_LIMIT)


def _log_sigmoid(x):
    return jnp.minimum(x, 0.0) - jnp.log1p(jnp.exp(-jnp.abs(x)))


def _sigmoid(x):
    return 0.5 * jnp.tanh(0.5 * x) + 0.5


def _silu(x):
    return x * _sigmoid(x)


def _norm_gate_kernel(x_ref, nw_ref, wg_ref, bg_ref, u_ref, g_ref):
    x = x_ref[...]
    r = lax.rsqrt(jnp.mean(x * x, axis=-1, keepdims=True) + EPS)
    ub = (x * r * nw_ref[...]).astype(BF16)
    u_ref[...] = ub
    wg = wg_ref[...].astype(BF16)
    wg = jnp.concatenate([wg, jnp.zeros((GATE_LANES - wg.shape[0], wg.shape[1]), BF16)], axis=0)
    g = lax.dot_general(ub, wg, (((1,), (1,)), ((), ())), preferred_element_type=F32) + bg_ref[...]
    lane = lax.broadcasted_iota(jnp.int32, g.shape, 1)
    g_ref[...] = jnp.where(lane >= N_HEADS, _log_sigmoid(g), g)


def _norm_gates(x, nw, w_t, layer, gate_row0, bg, tm=512):
    m, d = x.shape
    ng = 2 * N_HEADS
    assert gate_row0 % ng == 0 and w_t.shape[1] - gate_row0 == ng
    return pl.pallas_call(
        _norm_gate_kernel,
        grid=(m // tm,),
        in_specs=[
            pl.BlockSpec((tm, d), lambda i: (i, 0)),
            pl.BlockSpec((1, d), lambda i: (0, 0)),
            pl.BlockSpec((None, ng, d), lambda i: (layer, gate_row0 // ng, 0)),
            pl.BlockSpec((1, GATE_LANES), lambda i: (0, 0)),
        ],
        out_specs=[
            pl.BlockSpec((tm, d), lambda i: (i, 0)),
            pl.BlockSpec((tm, GATE_LANES), lambda i: (i, 0)),
        ],
        out_shape=[
            jax.ShapeDtypeStruct((m, d), BF16),
            jax.ShapeDtypeStruct((m, GATE_LANES), F32),
        ],
        compiler_params=_cparams(("arbitrary",)),
        name="norm_gates",
    )(x, nw, w_t, bg)


def _final_norm_kernel(n_prompt_blocks, x_ref, nw_ref, yp_ref, ys_ref):
    i = pl.program_id(0)
    x = x_ref[...]
    r = lax.rsqrt(jnp.mean(x * x, axis=-1, keepdims=True) + EPS)
    y = x * r * nw_ref[...]

    @pl.when(i < n_prompt_blocks)
    def _():
        yp_ref[...] = y

    @pl.when(i == n_prompt_blocks)
    def _():
        ys_ref[...] = y


def _final_norm(x, nw, m_p):
    m, d = x.shape
    tm = m - m_p
    assert m_p % tm == 0
    npb = m_p // tm
    return pl.pallas_call(
        functools.partial(_final_norm_kernel, npb),
        grid=(npb + 1,),
        in_specs=[pl.BlockSpec((tm, d), lambda i: (i, 0)),
                  pl.BlockSpec((1, d), lambda i: (0, 0))],
        out_specs=[pl.BlockSpec((tm, d), lambda i: (jnp.minimum(i, npb - 1), 0)),
                   pl.BlockSpec((tm, d), lambda i: (0, 0))],
        out_shape=[jax.ShapeDtypeStruct((m_p, d), F32), jax.ShapeDtypeStruct((tm, d), F32)],
        compiler_params=_cparams(("arbitrary",)),
        name="final_norm",
    )(x, nw)


def _in_proj_kernel(fill, u_ref, w_ref, o_ref, *rest):
    i = pl.program_id(1)
    if fill:
        zbuf_hbm, wb_ref, zeros_ref, sem = rest
        units, n_steps = fill
        tile = zeros_ref.shape[0]
        _, d1, d2 = zbuf_hbm.shape[:3]
        per_d2 = d2 // tile
        step = pl.program_id(0) * pl.num_programs(1) + i
        lo = step * units // n_steps
        hi = (step + 1) * units // n_steps

        def fill_copy(unit):
            dst = zbuf_hbm.at[unit // (d1 * per_d2), (unit // per_d2) % d1,
                              pl.ds((unit % per_d2) * tile, tile)]
            return pltpu.make_async_copy(zeros_ref, dst, sem.at[0])

        def for_own_units(fn):
            for t in range(-(-units // n_steps)):
                pl.when(lo + t < hi)(functools.partial(fn, lo + t))

        @pl.when(step == 0)
        def _():
            zeros_ref[...] = jnp.zeros_like(zeros_ref)

        for_own_units(lambda unit: fill_copy(unit).start())
    else:
        (wb_ref,) = rest

    @pl.when(i == 0)
    def _():
        wb_ref[...] = w_ref[...].astype(BF16)

    o_ref[...] = lax.dot_general(u_ref[...], wb_ref[...], (((1,), (1,)), ((), ())),
                                 preferred_element_type=F32).astype(o_ref.dtype)
    if fill:
        for_own_units(lambda unit: fill_copy(unit).wait())


def _in_proj(u, w_t, layer, col0, ncols, out_dtype, tm=544, tn=1024, zero_buffer_shape=None):
    m, k = u.shape
    off = col0 // tn
    grid = (ncols // tn, m // tm)
    out_specs = [pl.BlockSpec((tm, tn), lambda j, i: (i, j))]
    out_shape = [jax.ShapeDtypeStruct((m, ncols), out_dtype)]
    scratch = [pltpu.VMEM((tn, k), BF16)]
    fill = None
    if zero_buffer_shape is not None:
        a, b, h, r, c = zero_buffer_shape
        assert h % ZERO_TILE_HEADS == 0
        fill = (a * b * (h // ZERO_TILE_HEADS), grid[0] * grid[1])
        out_specs.append(pl.BlockSpec(memory_space=pl.ANY))
        out_shape.append(jax.ShapeDtypeStruct(zero_buffer_shape, F32))
        scratch += [pltpu.VMEM((ZERO_TILE_HEADS, r, c), F32), pltpu.SemaphoreType.DMA((1,))]
    outs = pl.pallas_call(
        functools.partial(_in_proj_kernel, fill),
        grid=grid,
        in_specs=[
            pl.BlockSpec((tm, k), lambda j, i: (i, 0)),
            pl.BlockSpec((None, tn, k), lambda j, i: (layer, j + off, 0)),
        ],
        out_specs=out_specs,
        out_shape=out_shape,
        scratch_shapes=scratch,
        compiler_params=_cparams(("arbitrary", "arbitrary")),
        name="in_proj",
    )(u, w_t)
    return outs[0] if fill is None else outs


def _merge_kernel(n_prompt_blocks, ha_ref, hbp_ref, hbs_ref, ga_ref, gb_ref, wa_ref, wb_ref,
                  o_ref, wab_ref, wbb_ref):
    i = pl.program_id(1)

    @pl.when(i == 0)
    def _():
        wab_ref[...] = wa_ref[...].astype(BF16)
        wbb_ref[...] = wb_ref[...].astype(BF16)

    hb = jnp.where(i >= n_prompt_blocks, hbs_ref[...], hbp_ref[...])
    pa = jnp.dot(ha_ref[...], wab_ref[...], preferred_element_type=F32)
    pb = jnp.dot(hb, wbb_ref[...], preferred_element_type=F32)
    ga = _sigmoid(ga_ref[...].astype(F32))
    gb = _sigmoid(gb_ref[...].astype(F32))
    o_ref[...] = (ga * pa + gb * pb).astype(o_ref.dtype)


def _merge(ha, hb_p, hb_s, proj_b, ga_col0, gb_col0, wa_all, wb_all, layer, tn=512):
    m, ka = ha.shape
    kb = hb_p.shape[1]
    n = wa_all.shape[2]
    tm = hb_s.shape[0]
    npb = hb_p.shape[0] // tm
    return pl.pallas_call(
        functools.partial(_merge_kernel, npb),
        grid=(n // tn, m // tm),
        in_specs=[
            pl.BlockSpec((tm, ka), lambda j, i: (i, 0)),
            pl.BlockSpec((tm, kb), lambda j, i: (jnp.minimum(i, npb - 1), 0)),
            pl.BlockSpec((tm, kb), lambda j, i: (0, 0)),
            pl.BlockSpec((tm, tn), lambda j, i: (i, j + ga_col0 // tn)),
            pl.BlockSpec((tm, tn), lambda j, i: (i, j + gb_col0 // tn)),
            pl.BlockSpec((None, ka, tn), lambda j, i: (layer, 0, j)),
            pl.BlockSpec((None, kb, tn), lambda j, i: (layer, 0, j)),
        ],
        out_specs=pl.BlockSpec((tm, tn), lambda j, i: (i, j)),
        out_shape=jax.ShapeDtypeStruct((m, n), BF16),
        scratch_shapes=[pltpu.VMEM((ka, tn), BF16), pltpu.VMEM((kb, tn), BF16)],
        compiler_params=_cparams(("arbitrary", "arbitrary")),
        name="merge",
    )(ha, hb_p, hb_s, proj_b, proj_b, wa_all, wb_all)


def _out_proj_kernel(a_ref, x_ref, w_ref, o_ref, wb_ref):
    @pl.when(pl.program_id(1) == 0)
    def _():
        wb_ref[...] = w_ref[...].astype(BF16)

    o_ref[...] = x_ref[...] + jnp.dot(a_ref[...], wb_ref[...], preferred_element_type=F32)


def _out_proj(a, x, w_all, layer, tm=1088, tn=512):
    m, k = a.shape
    n = x.shape[1]
    return pl.pallas_call(
        _out_proj_kernel,
        grid=(n // tn, m // tm),
        in_specs=[
            pl.BlockSpec((tm, k), lambda j, i: (i, 0)),
            pl.BlockSpec((tm, tn), lambda j, i: (i, j)),
            pl.BlockSpec((None, k, tn), lambda j, i: (layer, 0, j)),
        ],
        out_specs=pl.BlockSpec((tm, tn), lambda j, i: (i, j)),
        out_shape=jax.ShapeDtypeStruct((m, n), F32),
        scratch_shapes=[pltpu.VMEM((k, tn), BF16)],
        compiler_params=_cparams(("arbitrary", "arbitrary")),
        name="out_proj",
    )(a, x, w_all)


def _pool_kernel(tiles_per_seq, n_prompt_tiles, n_dec_seq,
                 xa_ref, prev_ref, za_ref, sp_ref, w_ref, sc_ref,
                 ha_ref, pp_ref, ps_ref, ext_ref, la_ref, lb_ref):
    g = pl.program_id(0)
    s = pl.program_id(1)
    tt = xa_ref.shape[0]
    wb = w_ref[...].astype(BF16)
    scale = sc_ref[...]

    def finish(pooled, za):
        y = jnp.dot(pooled.astype(BF16), wb, preferred_element_type=F32)
        return (y * scale * _silu(za)).astype(ha_ref.dtype)

    @pl.when(s < n_prompt_tiles)
    def _():
        i = s % tiles_per_seq
        xa = xa_ref[...]
        lo = 2 * POOL_PAD
        n = POOL_PAD + tt
        zeros = jnp.zeros((POOL_PAD, xa.shape[1]), F32)
        for ref in (ext_ref, la_ref, lb_ref):
            ref[0:POOL_PAD, :] = zeros
        ext_ref[POOL_PAD:lo, :] = jnp.where(i == 0, 0.0, prev_ref[...])
        ext_ref[lo:lo + tt, :] = xa
        pos = i * tt + lax.broadcasted_iota(jnp.int32, (tt, 1), 0)
        for gi, w in enumerate(POOL_WINDOWS):
            @pl.when(g == gi)
            def _(w=w):
                cur, h, level = ext_ref, 1, 0
                while h < w:
                    nxt = (la_ref, lb_ref)[level % 2]
                    nxt[POOL_PAD:POOL_PAD + n, :] = (cur[POOL_PAD:POOL_PAD + n, :]
                                                     + cur[POOL_PAD - h:POOL_PAD - h + n, :])
                    cur, h, level = nxt, 2 * h, level + 1
                inv = 1.0 / jnp.minimum(w, pos + 1).astype(F32)
                ha_ref[...] = finish(cur[lo:lo + tt, :] * inv - xa, za_ref[...])

        @pl.when(i == tiles_per_seq - 1)
        def _():
            pp_ref[0] = ext_ref[lo + tt - POOL_BUF:lo + tt, :]

    @pl.when(s == n_prompt_tiles)
    def _():
        nb = tt // n_dec_seq
        rows = [sp_ref[r] for r in range(POOL_BUF)]
        rows += [xa_ref[t * nb:(t + 1) * nb, :] for t in range(n_dec_seq)]
        for gi, w in enumerate(POOL_WINDOWS):
            @pl.when(g == gi)
            def _(w=w):
                for t in range(n_dec_seq):
                    acc = rows[POOL_BUF + t]
                    for j in range(1, w):
                        acc = acc + rows[POOL_BUF + t - j]
                    inv = 1.0 / float(min(w, PAST_LEN + t + 1))
                    pooled = acc * inv - rows[POOL_BUF + t]
                    ha_ref[t * nb:(t + 1) * nb, :] = finish(pooled, za_ref[t * nb:(t + 1) * nb, :])
        for r in range(POOL_BUF):
            ps_ref[r] = rows[r + n_dec_seq]


def _pool_mixer(proj_a, sp_t, w_pool, pool_scale3, layer, n_batch, seq, n_dec_seq, tt=512):
    m = proj_a.shape[0]
    w_a = proj_a.shape[1] // 2
    pg = w_a // len(POOL_WINDOWS)
    n_dec = sp_t.shape[2]
    assert n_dec * n_dec_seq == tt and seq % tt == 0 and m == n_batch * seq + tt
    tps = seq // tt
    npt = n_batch * tps
    ng = len(POOL_WINDOWS)
    return pl.pallas_call(
        functools.partial(_pool_kernel, tps, npt, n_dec_seq),
        grid=(ng, npt + 1),
        in_specs=[
            pl.BlockSpec((tt, pg), lambda g, s: (s, g)),
            pl.BlockSpec((POOL_PAD, pg),
                         lambda g, s: (jnp.maximum(s * (tt // POOL_PAD) - 1, 0), g)),
            pl.BlockSpec((tt, pg), lambda g, s: (s, ng + g)),
            pl.BlockSpec((None, POOL_BUF, n_dec, pg), lambda g, s: (layer, 0, 0, g)),
            pl.BlockSpec((None, None, pg, pg), lambda g, s: (layer, g, 0, 0)),
            pl.BlockSpec((None, 1, pg), lambda g, s: (layer, 0, g)),
        ],
        out_specs=[
            pl.BlockSpec((tt, pg), lambda g, s: (s, g)),
            pl.BlockSpec((1, POOL_BUF, pg), lambda g, s: (jnp.minimum(s, npt - 1) // tps, 0, g)),
            pl.BlockSpec((POOL_BUF, n_dec, pg), lambda g, s: (0, 0, g)),
        ],
        out_shape=[
            jax.ShapeDtypeStruct((m, w_a), BF16),
            jax.ShapeDtypeStruct((n_batch, POOL_BUF, w_a), F32),
            jax.ShapeDtypeStruct((POOL_BUF, n_dec, w_a), F32),
        ],
        scratch_shapes=[pltpu.VMEM((2 * POOL_PAD + tt, pg), F32)] * 3,
        compiler_params=_cparams(("arbitrary", "arbitrary")),
        name="pool_mixer",
    )(proj_a, proj_a, proj_a, sp_t, w_pool, pool_scale3)


SAMPLE_SUBSTEPS = 2
HEADS_PER_STEP = 2
SAMPLE_SEQS = 16


def _gate_math(ig_c, ig_r, lf_r, m0_c, mask, to_row):
    b_c = jnp.sum(jnp.where(mask, lf_r, 0.0), axis=1, keepdims=True)
    dmat = jnp.where(mask, b_c - to_row(b_c) + ig_r, -jnp.inf)
    m_state = b_c + m0_c
    m_t = jnp.maximum(m_state, jnp.max(dmat, axis=1, keepdims=True))
    wts = jnp.exp(dmat - m_t)
    s_state = jnp.exp(m_state - m_t)
    return b_c, m_t, wts, s_state


def _head_gates(h, o, zb, hw):
    h = _sigmoid(o) * h
    h = h * lax.rsqrt(jnp.mean(h * h, axis=-1, keepdims=True) + EPS)
    return h * hw * _silu(zb)


def _mlstm_kernel(n_sub, sq_ref, sk_ref, sv_ref, so_ref, sz_ref, sg_ref, sm0_ref, shw_ref, sc_in,
                  sn_in, c_acc_hbm,
                  q_ref, k_ref, v_ref, o_ref, zb_ref, g_ref, hw_ref,
                  hbs_ref, sc_out, sn_out, sm_out, hb_ref, c_out, n_out, m_out,
                  c_ref, n_ref, m_ref, gt_ref, kt_ref, wv_ref, scl_ref, qc_ref, a_ref, p_ref):
    del c_acc_hbm
    hd = pl.program_id(1)
    c = pl.program_id(2)
    nc = pl.num_programs(2)
    L = q_ref.shape[0]
    dqk = q_ref.shape[1] // HEADS_PER_STEP
    dv_p = v_ref.shape[1] // HEADS_PER_STEP

    @pl.when(c == 0)
    def _():
        c_ref[...] = jnp.zeros_like(c_ref)
        n_ref[...] = jnp.zeros_like(n_ref)
        m_ref[...] = jnp.zeros_like(m_ref)

    T, nb, _ = sq_ref.shape
    dv = sv_ref.shape[2]
    Ls = T * nb
    per_sub = nb // n_sub
    step = (pl.program_id(0) * pl.num_programs(1) + hd) * nc + c
    sub = step % n_sub
    shd = (step // n_sub) % N_HEADS

    def flat(ref):
        return ref[...].reshape(Ls, ref.shape[2])

    sq = flat(sq_ref)

    @pl.when(sub == 0)
    def _():
        sg = flat(sg_ref)
        slane = lax.broadcasted_iota(jnp.int32, sg.shape, 1)
        sig_c = jnp.sum(jnp.where(slane == shd, sg, 0.0), axis=1, keepdims=True)
        slf_c = jnp.sum(jnp.where(slane == shd + N_HEADS, sg, 0.0), axis=1, keepdims=True)
        sm0_c = jnp.sum(jnp.where(slane == shd, flat(sm0_ref), 0.0), axis=1, keepdims=True)
        ri = lax.broadcasted_iota(jnp.int32, (Ls, Ls), 0)
        ci = lax.broadcasted_iota(jnp.int32, (Ls, Ls), 1)
        smask = jnp.logical_and(ri % nb == ci % nb, ci <= ri)
        eye = ri == ci

        def s_to_row(col):
            return jnp.sum(jnp.where(eye, col, 0.0), axis=0, keepdims=True)

        sb_c, sm_t, swts, ss_state = _gate_math(sig_c, s_to_row(sig_c), s_to_row(slf_c), sm0_c,
                                                smask, s_to_row)
        sk = flat(sk_ref) * (dqk ** -0.5)
        sv = flat(sv_ref)
        skf = sk.astype(F32)
        n_old = sn_in[...]
        sqk = lax.dot_general(sq, sk, (((1,), (1,)), ((), ())), preferred_element_type=F32) * swts
        pv = jnp.dot(sqk.astype(BF16), sv, preferred_element_type=F32)
        qn = jnp.sum(sq.astype(F32) * jnp.concatenate([n_old] * T, axis=0), axis=1, keepdims=True)
        sden = ss_state * qn + jnp.sum(sqk, axis=1, keepdims=True)
        inv = 1.0 / jnp.maximum(jnp.abs(sden), jnp.exp(-sm_t))
        a_ref[...] = ss_state * inv
        p_ref[...] = pv * inv
        qc_ref[...] = jnp.zeros_like(qc_ref)

        last = slice((T - 1) * nb, T * nb)
        sb_last = sb_c[last, :]
        sm_new = sm_t[last, :]
        scl = jnp.exp(sb_last + sm0_c[last, :] - sm_new)
        rep = lambda x: jnp.concatenate([x] * T, axis=0)
        sws = jnp.exp(rep(sb_last) - sb_c + sig_c - rep(sm_new))
        wv_ref[...] = (sws * sv.astype(F32)).astype(BF16)
        wk = sws * skf
        n_upd = wk[0:nb, :]
        for t in range(1, T):
            n_upd = n_upd + wk[t * nb:(t + 1) * nb, :]
        sn_out[...] = scl * n_old + n_upd
        sm_out[...] = sm_new
        scl_ref[...] = scl
        kt_ref[...] = skf.T.astype(BF16)

    g = g_ref[...]
    lane = lax.broadcasted_iota(jnp.int32, g.shape, 1)
    gt_ref[...] = g.T
    ti = lax.broadcasted_iota(jnp.int32, (L, L), 0)
    si = lax.broadcasted_iota(jnp.int32, (L, L), 1)

    def to_row(col):
        return jnp.broadcast_to(col, (L, GATE_LANES)).T[0:1, :]

    for hh in range(HEADS_PER_STEP):
        hd = pl.program_id(1) * HEADS_PER_STEP + hh
        qs = slice(hh * dqk, (hh + 1) * dqk)
        vs = slice(hh * dv_p, (hh + 1) * dv_p)
        ig_c = jnp.sum(jnp.where(lane == hd, g, 0.0), axis=1, keepdims=True)
        ig_r = gt_ref[pl.ds(hd, 1), :]
        lf_r = gt_ref[pl.ds(hd + N_HEADS, 1), :]
        m0 = m_ref[hh]
        b_c, m_t, wts, s_state = _gate_math(ig_c, ig_r, lf_r, m0, si <= ti, to_row)

        q = q_ref[:, qs]
        k = k_ref[:, qs] * (dqk ** -0.5)
        v = v_ref[:, vs]
        cmat = c_ref[hh]
        nvec = n_ref[hh]
        s_qk = lax.dot_general(q, k, (((1,), (1,)), ((), ())), preferred_element_type=F32) * wts
        num = (s_state * jnp.dot(q, cmat.astype(BF16), preferred_element_type=F32)
               + jnp.dot(s_qk.astype(BF16), v, preferred_element_type=F32))
        den = (s_state * jnp.sum(q.astype(F32) * nvec, axis=1, keepdims=True)
               + jnp.sum(s_qk, axis=1, keepdims=True))
        hval = num / jnp.maximum(jnp.abs(den), jnp.exp(-m_t))
        hb_ref[:, vs] = _head_gates(hval, o_ref[:, vs].astype(F32), zb_ref[:, vs].astype(F32),
                                    hw_ref[:, vs]).astype(hb_ref.dtype)

        b_last = b_c[L - 1:L, :]
        m_new = m_t[L - 1:L, :]
        ws = jnp.exp(b_last - b_c + ig_c - m_new)
        sc = jnp.exp(b_last + m0 - m_new)
        wv = (ws * v.astype(F32)).astype(BF16)
        c_ref[hh] = sc * cmat + lax.dot_general(k, wv, (((0,), (0,)), ((), ())),
                                                preferred_element_type=F32)
        n_ref[hh] = sc * nvec + jnp.sum(ws * k.astype(F32), axis=0, keepdims=True)
        m_ref[hh] = m_new

    seq_of_row = lax.broadcasted_iota(jnp.int32, (Ls, 1), 0) % nb
    seq_of_col = lax.broadcasted_iota(jnp.int32, (1, Ls), 1) % nb
    kt = kt_ref[...]
    swv = wv_ref[...]
    qc = qc_ref[...]
    for jl in range(per_sub):
        j = sub * per_sub + jl
        cj = sc_in[jl]
        qj = jnp.where(seq_of_row == j, sq, jnp.zeros_like(sq))
        qc = qc + jnp.dot(qj, cj.astype(BF16), preferred_element_type=F32)
        ktj = jnp.where(seq_of_col == j, kt, jnp.zeros_like(kt))
        sc_out[jl] = scl_ref[pl.ds(j, 1), :] * cj + jnp.dot(ktj, swv, preferred_element_type=F32)
    qc_ref[...] = qc

    @pl.when(c == nc - 1)
    def _():
        c_out[...] = c_ref[...]
        n_out[...] = n_ref[...]
        m_out[...] = m_ref[...]

    @pl.when(sub == n_sub - 1)
    def _():
        hs = a_ref[...] * qc + p_ref[...]
        hs = _head_gates(hs, flat(so_ref).astype(F32), flat(sz_ref).astype(F32), shw_ref[...])
        hbs_ref[...] = hs.astype(hbs_ref.dtype).reshape(T, nb, dv)


def _mlstm(proj_b, gates, m0_3, head_w3, c_all, n_t, c_acc, layer, n_batch, seq, dqk, dv, cols,
           L=256):
    n_dec = n_t.shape[2]
    T = m0_3.shape[0]
    H, G, nbs, HP = N_HEADS, SAMPLE_SUBSTEPS, SAMPLE_SEQS, HEADS_PER_STEP
    HG = H // HP
    nc = seq // L
    rows = n_batch * seq
    m_tot = proj_b.shape[0]
    assert m_tot == rows + T * n_dec and rows % (T * n_dec) == 0 and nc % G == 0
    assert n_batch * HG * nc == (n_dec // nbs) * H * G
    q0, k0, v0, o0, z0 = (cols[n] for n in ("q", "k", "v", "o", "zb"))
    proj_b3 = proj_b.reshape(m_tot // n_dec, n_dec, proj_b.shape[1])
    gates3 = gates.reshape(m_tot // n_dec, n_dec, GATE_LANES)
    tb = rows // n_dec // T
    per_sub = nbs // G

    def rb(b, h, c):
        return b * nc + c

    def sblk(b, h, c):
        return (b * HG + h) * (nc // G) + c // G

    def si(b, h, c):
        return sblk(b, h, c) // H

    def sh(b, h, c):
        return sblk(b, h, c) % H

    def cidx(b, h, c):
        return (layer, si(b, h, c) * G + c % G, sh(b, h, c), 0, 0)

    return pl.pallas_call(
        functools.partial(_mlstm_kernel, G),
        grid=(n_batch, HG, nc),
        in_specs=[
            pl.BlockSpec((T, nbs, dqk), lambda b, h, c: (tb, si(b, h, c), q0 // dqk + sh(b, h, c))),
            pl.BlockSpec((T, nbs, dqk), lambda b, h, c: (tb, si(b, h, c), k0 // dqk + sh(b, h, c))),
            pl.BlockSpec((T, nbs, dv), lambda b, h, c: (tb, si(b, h, c), v0 // dv + sh(b, h, c))),
            pl.BlockSpec((T, nbs, dv), lambda b, h, c: (tb, si(b, h, c), o0 // dv + sh(b, h, c))),
            pl.BlockSpec((T, nbs, dv), lambda b, h, c: (tb, si(b, h, c), z0 // dv + sh(b, h, c))),
            pl.BlockSpec((T, nbs, GATE_LANES), lambda b, h, c: (tb, si(b, h, c), 0)),
            pl.BlockSpec((T, nbs, GATE_LANES), lambda b, h, c: (0, si(b, h, c), 0)),
            pl.BlockSpec((None, 1, dv), lambda b, h, c: (layer, 0, sh(b, h, c))),
            pl.BlockSpec((None, per_sub, None, dqk, dv), cidx),
            pl.BlockSpec((None, None, nbs, dqk), lambda b, h, c: (layer, sh(b, h, c), si(b, h, c), 0)),
            pl.BlockSpec(memory_space=pl.ANY),
            pl.BlockSpec((L, HP * dqk), lambda b, h, c: (rb(b, h, c), q0 // (HP * dqk) + h)),
            pl.BlockSpec((L, HP * dqk), lambda b, h, c: (rb(b, h, c), k0 // (HP * dqk) + h)),
            pl.BlockSpec((L, HP * dv), lambda b, h, c: (rb(b, h, c), v0 // (HP * dv) + h)),
            pl.BlockSpec((L, HP * dv), lambda b, h, c: (rb(b, h, c), o0 // (HP * dv) + h)),
            pl.BlockSpec((L, HP * dv), lambda b, h, c: (rb(b, h, c), z0 // (HP * dv) + h)),
            pl.BlockSpec((L, GATE_LANES), lambda b, h, c: (rb(b, h, c), 0)),
            pl.BlockSpec((None, 1, HP * dv), lambda b, h, c: (layer, 0, h)),
        ],
        out_specs=[
            pl.BlockSpec((T, nbs, dv), lambda b, h, c: (0, si(b, h, c), sh(b, h, c))),
            pl.BlockSpec((None, per_sub, None, dqk, dv), cidx),
            pl.BlockSpec((None, nbs, dqk), lambda b, h, c: (sh(b, h, c), si(b, h, c), 0)),
            pl.BlockSpec((None, nbs, 1), lambda b, h, c: (sh(b, h, c), si(b, h, c), 0)),
            pl.BlockSpec((L, HP * dv), lambda b, h, c: (rb(b, h, c), h)),
            pl.BlockSpec((None, HP, dqk, dv), lambda b, h, c: (b, h, 0, 0)),
            pl.BlockSpec((None, HP, 1, dqk), lambda b, h, c: (b, h, 0, 0)),
            pl.BlockSpec((None, HP, 1, 1), lambda b, h, c: (b, h, 0, 0)),
        ],
        out_shape=[
            jax.ShapeDtypeStruct((T, n_dec, H * dv), BF16),
            jax.ShapeDtypeStruct(c_acc.shape, F32),
            jax.ShapeDtypeStruct((H, n_dec, dqk), F32),
            jax.ShapeDtypeStruct((H, n_dec, 1), F32),
            jax.ShapeDtypeStruct((rows, H * dv), BF16),
            jax.ShapeDtypeStruct((n_batch, H, dqk, dv), F32),
            jax.ShapeDtypeStruct((n_batch, H, 1, dqk), F32),
            jax.ShapeDtypeStruct((n_batch, H, 1, 1), F32),
        ],
        scratch_shapes=[
            pltpu.VMEM((HP, dqk, dv), F32), pltpu.VMEM((HP, 1, dqk), F32),
            pltpu.VMEM((HP, 1, 1), F32),
            pltpu.VMEM((GATE_LANES, L), F32),
            pltpu.VMEM((dqk, T * nbs), BF16), pltpu.VMEM((T * nbs, dv), BF16),
            pltpu.VMEM((nbs, 1), F32), pltpu.VMEM((T * nbs, dv), F32),
            pltpu.VMEM((T * nbs, 1), F32), pltpu.VMEM((T * nbs, dv), F32),
        ],
        input_output_aliases={10: 1},
        compiler_params=_cparams(("arbitrary", "arbitrary", "arbitrary")),
        name="mlstm",
    )(proj_b3, proj_b3, proj_b3, proj_b3, proj_b3, gates3, m0_3, head_w3, c_all, n_t, c_acc,
      proj_b, proj_b, proj_b, proj_b, proj_b, gates, head_w3)


def kernel(x_prompt, x_sample, state_pool, state_C, state_n, state_m, norm_w, w_in, b_gate, w_pool,
           pool_scale, head_norm_w, w_proj_a, w_proj_b, w_out, final_norm_w):
    n_batch, seq, d = x_prompt.shape
    n_dec, dec_seq, _ = x_sample.shape
    depth = norm_w.shape[0]
    w_a = state_pool.shape[3]
    dqk, dv = state_C.shape[3], state_C.shape[4]
    w_qk, w_b = N_HEADS * dqk, N_HEADS * dv
    n_main = 2 * w_a + 2 * w_qk + 3 * w_b + 2 * d
    assert w_in.shape[2] == n_main + 2 * N_HEADS
    n_a = 2 * w_a
    names = ("q", "k", "v", "o", "zb", "ga", "gb")
    sizes = (w_qk, w_qk, w_b, w_b, w_b, d, d)
    cols, c0 = {}, 0
    for nme, sz in zip(names, sizes):
        cols[nme] = c0
        c0 += sz
    m_p = n_batch * seq
    m_s = n_dec * dec_seq
    m = m_p + m_s

    x = jnp.concatenate([x_prompt.reshape(m_p, d),
                         x_sample.transpose(1, 0, 2).reshape(m_s, d)], axis=0)
    w_t = w_in.transpose(0, 2, 1)
    bg = jnp.pad(b_gate, ((0, 0), (0, GATE_LANES - 2 * N_HEADS)))[:, None, :]
    sp_t = state_pool.transpose(0, 2, 1, 3)
    n_t = state_n.transpose(0, 2, 1, 3)
    m0_all = jnp.pad(jnp.tile(state_m[:, None], (1, dec_seq, 1, 1)),
                     ((0, 0), (0, 0), (0, 0), (0, GATE_LANES - N_HEADS)))
    pool_scale3 = pool_scale[:, None, :]
    head_w3 = head_norm_w[:, None, :]
    norm_w3 = norm_w[:, None, :]

    pool_p, c_p, n_p, m_pr = [], [], [], []
    pool_s, n_s, m_sm = [], [], []
    for l in range(depth):
        u, gates = _norm_gates(x, norm_w3[l], w_t, l, n_main, bg[l])
        proj_a = _in_proj(u, w_t, l, 0, n_a, F32, tm=1088, tn=512)
        if l == 0:
            proj_b, c_s = _in_proj(u, w_t, l, n_a, n_main - n_a, BF16,
                                   zero_buffer_shape=state_C.shape)
        else:
            proj_b = _in_proj(u, w_t, l, n_a, n_main - n_a, BF16)
        ha, pp, ps = _pool_mixer(proj_a, sp_t, w_pool, pool_scale3, l, n_batch, seq, dec_seq)
        hb_s, c_s, ns, ms, hb_p, cp, np_, mp = _mlstm(
            proj_b, gates, m0_all[l], head_w3, state_C, n_t, c_s, l, n_batch, seq, dqk, dv, cols)
        merged = _merge(ha, hb_p, hb_s.reshape(m_s, w_b), proj_b, cols["ga"], cols["gb"],
                        w_proj_a, w_proj_b, l)
        x = _out_proj(merged, x, w_out, l)
        pool_p.append(pp); c_p.append(cp); n_p.append(np_.reshape(n_batch, N_HEADS, dqk))
        m_pr.append(mp.reshape(n_batch, N_HEADS))
        pool_s.append(ps.transpose(1, 0, 2)); n_s.append(ns.transpose(1, 0, 2))
        m_sm.append(ms.reshape(N_HEADS, n_dec).T)

    y_p, y_s = _final_norm(x, final_norm_w[None, :], m_p)
    y_prompt = y_p.reshape(n_batch, seq, d)
    y_sample = y_s.reshape(dec_seq, n_dec, d).transpose(1, 0, 2)
    return (y_prompt, y_sample,
            jnp.stack(pool_p), jnp.stack(c_p), jnp.stack(n_p), jnp.stack(m_pr),
            jnp.stack(pool_s), c_s, jnp.stack(n_s), jnp.stack(m_sm))
```
